```python
import math
import jax, jax.numpy as jnp
from jax import lax
import numpy as np

D_MODEL = 1024
BATCH = 2
SEQ = 8192
DEPTH = 2

N_MIXERS = 2
N_MEM = 256
HEAD_DIM = 64
MIX_WIDTH = D_MODEL
MEM_HEADS = 4
MEM_WIDTH = MEM_HEADS * HEAD_DIM
TOK_WIDTH = MIX_WIDTH - MEM_WIDTH
DIFF_HEADS = TOK_WIDTH // (2 * HEAD_DIM)
DIFF_V_DIM = 2 * HEAD_DIM
DIFF_WIDTH = DIFF_HEADS * DIFF_V_DIM
QK_A = DIFF_HEADS * 2 * HEAD_DIM
WIN_Q_HEADS = TOK_WIDTH // HEAD_DIM
WIN_GROUP = 3
WIN_KV_HEADS = WIN_Q_HEADS // WIN_GROUP
WIN_Q_WIDTH = WIN_Q_HEADS * HEAD_DIM
WIN_KV_WIDTH = WIN_KV_HEADS * HEAD_DIM
WIN_WIDTH = WIN_Q_WIDTH
WINDOW = 128
BLOCK = 128
D_FF = 2816
A_IN = 2 * QK_A + DIFF_WIDTH + MEM_WIDTH
B_IN = WIN_Q_WIDTH + 2 * WIN_KV_WIDTH + MEM_WIDTH
N_A = (DEPTH + 1) // 2
N_B = DEPTH // 2
DN_ALPHA = (2 * DEPTH) ** 0.25
DN_BETA = (8 * DEPTH) ** -0.25
LN_EPS = 1e-5
NEG = -1e30

kernel_name = 'hybrid_diffattn_swa_gqa_memory_macaron_deepnorm'


def layer_norm(x, g, b):
    xf = x.astype(jnp.float32)
    mu = jnp.mean(xf, -1, keepdims=True)
    var = jnp.mean(jnp.square(xf - mu), -1, keepdims=True)
    return ((xf - mu) * lax.rsqrt(var + LN_EPS) * g.astype(jnp.float32) + b.astype(jnp.float32)).astype(x.dtype)


def swiglu(x, w13, w2):
    gate, up = jnp.split(x @ w13, 2, axis=-1)
    return (jax.nn.silu(gate) * up) @ w2


def alibi_slopes(n):
    return jnp.asarray(2.0 ** (-8.0 * np.arange(1, n + 1) / n), dtype=jnp.float32)


def memory_attention(qm, mem, w_mem_kv):
    b, s, _ = qm.shape
    m = mem.shape[1]
    kv = (mem @ w_mem_kv).reshape(b, m, 2, MEM_HEADS, HEAD_DIM)
    k, v = kv[:, :, 0], kv[:, :, 1]
    q = qm.reshape(b, s, MEM_HEADS, HEAD_DIM)
    sc = jnp.einsum('bshd,bmhd->bhsm', q, k).astype(jnp.float32) * (HEAD_DIM ** -0.5)
    p = jax.nn.softmax(sc, axis=-1).astype(v.dtype)
    return jnp.einsum('bhsm,bmhd->bshd', p, v).reshape(b, s, MEM_WIDTH)


def diff_attention(q, k, v, lam, lam_init, subln_g):
    b, s = q.shape[:2]
    nb = s // BLOCK
    slopes = alibi_slopes(DIFF_HEADS)
    kpos = jnp.arange(s)
    scale = HEAD_DIM ** -0.5
    qb = q.reshape(b, nb, BLOCK, DIFF_HEADS, 2, HEAD_DIM).swapaxes(0, 1)

    def block(args):
        qj, j = args
        sc = jnp.einsum('bqhcd,bkhcd->bhcqk', qj, k).astype(jnp.float32) * scale
        qpos = j * BLOCK + jnp.arange(BLOCK)
        dist = jnp.abs(qpos[:, None] - kpos[None, :]).astype(jnp.float32)
        p = jax.nn.softmax(sc - slopes[:, None, None, None] * dist, axis=-1)
        a = p[:, :, 0] - lam * p[:, :, 1]
        return jnp.einsum('bhqk,bkhe->bqhe', a.astype(v.dtype), v)

    o = lax.map(block, (qb, jnp.arange(nb)))
    of = o.swapaxes(0, 1).reshape(b, s, DIFF_HEADS, DIFF_V_DIM).astype(jnp.float32)
    of = of * lax.rsqrt(jnp.mean(of * of, -1, keepdims=True) + LN_EPS) * subln_g.astype(jnp.float32)
    return (of * (1.0 - lam_init)).astype(v.dtype).reshape(b, s, DIFF_WIDTH)


def window_attention(q, k, v, sink):
    b, s = q.shape[:2]
    nb = s // BLOCK
    scale = HEAD_DIM ** -0.5
    qb = q.reshape(b, nb, BLOCK, WIN_KV_HEADS, WIN_GROUP, HEAD_DIM)

    def band(t):
        tp = jnp.pad(t, ((0, 0), (BLOCK, BLOCK), (0, 0), (0, 0)))
        tp = tp.reshape(b, nb + 2, BLOCK, WIN_KV_HEADS, HEAD_DIM)
        return jnp.concatenate([tp[:, 0:nb], tp[:, 1:nb + 1], tp[:, 2:nb + 2]], axis=2)

    kb, vb = band(k), band(v)
    r = jnp.arange(3 * BLOCK)
    delta = jnp.arange(BLOCK)[:, None] + BLOCK - r[None, :]
    kpos = jnp.arange(nb)[:, None] * BLOCK - BLOCK + r[None, :]
    valid = (jnp.abs(delta) <= WINDOW)[None] & ((kpos >= 0) & (kpos < s))[:, None, :]
    slopes = alibi_slopes(WIN_Q_HEADS).reshape(WIN_KV_HEADS, WIN_GROUP)
    sc = jnp.einsum('bnqkgd,bnrkd->bnkgqr', qb, kb).astype(jnp.float32) * scale
    sc = sc - slopes[:, :, None, None] * jnp.abs(delta).astype(jnp.float32)
    sc = jnp.where(valid[None, :, None, None], sc, NEG)
    sink_l = jnp.broadcast_to(sink.astype(jnp.float32).reshape(1, 1, WIN_KV_HEADS, WIN_GROUP, 1, 1), sc.shape[:-1] + (1,))
    p = jax.nn.softmax(jnp.concatenate([sc, sink_l], axis=-1), axis=-1)[..., :-1]
    o = jnp.einsum('bnkgqr,bnrkd->bnqkgd', p.astype(v.dtype), vb)
    return o.reshape(b, s, WIN_WIDTH)


def mixer_a(x, mem, w_in, w_mem_kv, w_out, lq1, lk1, lq2, lk2, subln_g, layer_idx):
    b, s, _ = x.shape
    h = x @ w_in
    q, k, v, qm = jnp.split(h, [QK_A, 2 * QK_A, 2 * QK_A + DIFF_WIDTH], axis=-1)
    q = q.reshape(b, s, DIFF_HEADS, 2, HEAD_DIM)
    k = k.reshape(b, s, DIFF_HEADS, 2, HEAD_DIM)
    v = v.reshape(b, s, DIFF_HEADS, DIFF_V_DIM)
    lam_init = 0.8 - 0.6 * math.exp(-0.3 * layer_idx)
    f32 = jnp.float32
    lam = (jnp.exp(jnp.sum(lq1.astype(f32) * lk1.astype(f32)))
           - jnp.exp(jnp.sum(lq2.astype(f32) * lk2.astype(f32))) + lam_init)
    o = diff_attention(q, k, v, lam, lam_init, subln_g)
    m = memory_attention(qm, mem, w_mem_kv)
    return jnp.concatenate([o, m], axis=-1) @ w_out


def mixer_b(x, mem, w_in, w_mem_kv, w_out, sink):
    b, s, _ = x.shape
    h = x @ w_in
    q, k, v, qm = jnp.split(h, [WIN_Q_WIDTH, WIN_Q_WIDTH + WIN_KV_WIDTH, WIN_Q_WIDTH + 2 * WIN_KV_WIDTH], axis=-1)
    q = q.reshape(b, s, WIN_Q_HEADS, HEAD_DIM)
    k = k.reshape(b, s, WIN_KV_HEADS, HEAD_DIM)
    v = v.reshape(b, s, WIN_KV_HEADS, HEAD_DIM)
    o = window_attention(q, k, v, sink)
    m = memory_attention(qm, mem, w_mem_kv)
    return jnp.concatenate([o, m], axis=-1) @ w_out


def setup_inputs(seed: int = 0) -> dict:
    key = jax.random.key(seed)
    ks = jax.random.split(key, 24)
    f = jnp.float32

    def nrm(k, shape, scale):
        return jax.random.normal(k, shape, f) * scale

    def gain(k, shape):
        return 1.0 + nrm(k, shape, 0.02)

    a_col = jnp.concatenate([jnp.ones((2 * QK_A,), f), jnp.full((DIFF_WIDTH,), DN_BETA, f), jnp.ones((MEM_WIDTH,), f)])
    b_col = jnp.concatenate([jnp.ones((WIN_Q_WIDTH + WIN_KV_WIDTH,), f), jnp.full((WIN_KV_WIDTH,), DN_BETA, f), jnp.ones((MEM_WIDTH,), f)])
    kv_col = jnp.concatenate([jnp.ones((MEM_WIDTH,), f), jnp.full((MEM_WIDTH,), DN_BETA, f)])
    return {
        'x': nrm(ks[0], (BATCH, SEQ, D_MODEL), 1.0),
        'mem': nrm(ks[1], (BATCH, N_MEM, D_MODEL), 1.0),
        'ffn1_w13': nrm(ks[2], (DEPTH, D_MODEL, 2 * D_FF), D_MODEL ** -0.5),
        'ffn1_w2': nrm(ks[3], (DEPTH, D_FF, D_MODEL), D_FF ** -0.5 * DN_BETA),
        'ln1_g': gain(ks[4], (DEPTH, D_MODEL)),
        'ln1_b': nrm(ks[5], (DEPTH, D_MODEL), 0.02),
        'w_mem_kv': nrm(ks[6], (DEPTH, D_MODEL, 2 * MEM_WIDTH), D_MODEL ** -0.5) * kv_col,
        'w_out': nrm(ks[7], (DEPTH, MIX_WIDTH, D_MODEL), MIX_WIDTH ** -0.5 * DN_BETA),
        'ln2_g': gain(ks[8], (DEPTH, D_MODEL)),
        'ln2_b': nrm(ks[9], (DEPTH, D_MODEL), 0.02),
        'ffn2_w13': nrm(ks[10], (DEPTH, D_MODEL, 2 * D_FF), D_MODEL ** -0.5),
        'ffn2_w2': nrm(ks[11], (DEPTH, D_FF, D_MODEL), D_FF ** -0.5 * DN_BETA),
        'ln3_g': gain(ks[12], (DEPTH, D_MODEL)),
        'ln3_b': nrm(ks[13], (DEPTH, D_MODEL), 0.02),
        'a_w_in': nrm(ks[14], (N_A, D_MODEL, A_IN), D_MODEL ** -0.5) * a_col,
        'a_lambda_q1': nrm(ks[15], (N_A, HEAD_DIM), 0.1),
        'a_lambda_k1': nrm(ks[16], (N_A, HEAD_DIM), 0.1),
        'a_lambda_q2': nrm(ks[17], (N_A, HEAD_DIM), 0.1),
        'a_lambda_k2': nrm(ks[18], (N_A, HEAD_DIM), 0.1),
        'a_subln_g': gain(ks[19], (N_A, DIFF_V_DIM)),
        'b_w_in': nrm(ks[20], (N_B, D_MODEL, B_IN), D_MODEL ** -0.5) * b_col,
        'b_sink': nrm(ks[21], (N_B, WIN_Q_HEADS), 0.5),
    }


def reference(x, mem, ffn1_w13, ffn1_w2, ln1_g, ln1_b, w_mem_kv, w_out, ln2_g, ln2_b,
              ffn2_w13, ffn2_w2, ln3_g, ln3_b, a_w_in, a_lambda_q1, a_lambda_k1,
              a_lambda_q2, a_lambda_k2, a_subln_g, b_w_in, b_sink):
    for i in range(DEPTH):
        x = layer_norm(DN_ALPHA * x + 0.5 * swiglu(x, ffn1_w13[i], ffn1_w2[i]), ln1_g[i], ln1_b[i])
        j = i // N_MIXERS
        if i % N_MIXERS == 0:
            y = mixer_a(x, mem, a_w_in[j], w_mem_kv[i], w_out[i], a_lambda_q1[j], a_lambda_k1[j],
                        a_lambda_q2[j], a_lambda_k2[j], a_subln_g[j], i)
        else:
            y = mixer_b(x, mem, b_w_in[j], w_mem_kv[i], w_out[i], b_sink[j])
        x = layer_norm(DN_ALPHA * x + y, ln2_g[i], ln2_b[i])
        x = layer_norm(DN_ALPHA * x + 0.5 * swiglu(x, ffn2_w13[i], ffn2_w2[i]), ln3_g[i], ln3_b[i])
    return x
```

```python
import functools
import math

import numpy as np
import jax
import jax.numpy as jnp
from jax import lax
from jax.experimental import pallas as pl
from jax.experimental.pallas import tpu as pltpu

D_MODEL = 1024
DEPTH = 2
N_MIXERS = 2
N_MEM = 256
HEAD_DIM = 64
MEM_HEADS = 4
MEM_WIDTH = MEM_HEADS * HEAD_DIM
TOK_WIDTH = D_MODEL - MEM_WIDTH
DIFF_HEADS = TOK_WIDTH // (2 * HEAD_DIM)
DIFF_V_DIM = 2 * HEAD_DIM
QK_A = DIFF_HEADS * 2 * HEAD_DIM
WIN_Q_HEADS = TOK_WIDTH // HEAD_DIM
WIN_GROUP = 3
WIN_KV_HEADS = WIN_Q_HEADS // WIN_GROUP
WIN_KV_WIDTH = WIN_KV_HEADS * HEAD_DIM
WINDOW = 128
D_FF = 2816
A_IN = 2 * QK_A + TOK_WIDTH + MEM_WIDTH
B_IN = TOK_WIDTH + 2 * WIN_KV_WIDTH + MEM_WIDTH
N_A = (DEPTH + 1) // 2
DN_ALPHA = (2 * DEPTH) ** 0.25
LN_EPS = 1e-5
NEG = -1e30
QK_SCALE = HEAD_DIM ** -0.5

F32 = jnp.float32
BF16 = jnp.bfloat16

V7X_LANES = 128
V7X_MXU_DIM = 256
V7X_VMEM_BYTES = 64 * 1024 * 1024

FFN_TM = 512
FF_CHUNK = V7X_MXU_DIM
PROJ_TM = 1024
PROJ_CHUNK = V7X_MXU_DIM
DIFF_TQ = 256
DIFF_TK = 512
MIX_TM = 512
WIN_TQ = 512


def _vmem_limit(nbytes):
    return int(min(V7X_VMEM_BYTES - 8 * 1024 * 1024, nbytes * 3 // 2))


def _resident(block_shape, index_map):
    return pl.BlockSpec(block_shape, index_map, pipeline_mode=pl.Buffered(1))


def _layer_norm(y, g, b):
    mu = jnp.mean(y, axis=-1, keepdims=True)
    yc = y - mu
    var = jnp.mean(yc * yc, axis=-1, keepdims=True)
    return yc * lax.rsqrt(var + LN_EPS) * g + b


def _alibi_slopes(n):
    return jnp.asarray(2.0 ** (-8.0 * np.arange(1, n + 1) / n), dtype=F32)


def _ffn_ln_kernel(x_ref, wg_ref, wu_ref, w2_ref, g_ref, b_ref, o_ref, acc_ref):
    x = x_ref[...]
    xb = x.astype(BF16)
    for c in range(D_FF // FF_CHUNK):
        cols = slice(c * FF_CHUNK, (c + 1) * FF_CHUNK)
        gate = jnp.dot(xb, wg_ref[:, cols], preferred_element_type=F32)
        up = jnp.dot(xb, wu_ref[:, cols], preferred_element_type=F32)
        act = (gate / (1.0 + jnp.exp(-gate)) * up).astype(BF16)
        part = jnp.dot(act, w2_ref[cols, :], preferred_element_type=F32)
        if c == 0:
            acc_ref[...] = part
        else:
            acc_ref[...] += part
    y = DN_ALPHA * x + 0.5 * acc_ref[...]
    o_ref[...] = _layer_norm(y, g_ref[...], b_ref[...])


def _ffn_ln(x, w13, w2, g, b):
    n = x.shape[0]
    w13 = w13.astype(BF16)
    w2 = w2.astype(BF16)
    est = (2 * D_MODEL * D_FF + D_FF * D_MODEL) * 2 + 5 * FFN_TM * D_MODEL * 4 + 4 * FFN_TM * FF_CHUNK * 4
    return pl.pallas_call(
        _ffn_ln_kernel,
        out_shape=jax.ShapeDtypeStruct((n, D_MODEL), F32),
        grid=(n // FFN_TM,),
        in_specs=[
            pl.BlockSpec((FFN_TM, D_MODEL), lambda i: (i, 0)),
            _resident((D_MODEL, D_FF), lambda i: (0, 0)),
            _resident((D_MODEL, D_FF), lambda i: (0, 1)),
            _resident((D_FF, D_MODEL), lambda i: (0, 0)),
            _resident((1, D_MODEL), lambda i: (0, 0)),
            _resident((1, D_MODEL), lambda i: (0, 0)),
        ],
        out_specs=pl.BlockSpec((FFN_TM, D_MODEL), lambda i: (i, 0)),
        scratch_shapes=[pltpu.VMEM((FFN_TM, D_MODEL), F32)],
        compiler_params=pltpu.CompilerParams(
            dimension_semantics=("arbitrary",), vmem_limit_bytes=_vmem_limit(est)),
        name="ffn_ln",
    )(x, w13, w13, w2, g.reshape(1, D_MODEL), b.reshape(1, D_MODEL))


def _in_proj_a_kernel(x_ref, w_ref, qk_ref, vt_ref, qm_ref):
    xb = x_ref[...].astype(BF16)
    n_qk = 2 * QK_A // PROJ_CHUNK
    n_v = TOK_WIDTH // PROJ_CHUNK
    for c in range(A_IN // PROJ_CHUNK):
        cols = slice(c * PROJ_CHUNK, (c + 1) * PROJ_CHUNK)
        r = jnp.dot(xb, w_ref[:, cols], preferred_element_type=F32)
        if c < QK_A // PROJ_CHUNK:
            qk_ref[:, cols] = (r * QK_SCALE).astype(BF16)
        elif c < n_qk:
            qk_ref[:, cols] = r.astype(BF16)
        elif c < n_qk + n_v:
            rt = r.T.astype(BF16)
            rows = slice((c - n_qk) * PROJ_CHUNK, (c - n_qk + 1) * PROJ_CHUNK)
            for t in range(PROJ_TM // DIFF_TK):
                vt_ref[t, rows, :] = rt[:, t * DIFF_TK:(t + 1) * DIFF_TK]
        else:
            qm_ref[...] = (r * QK_SCALE).astype(BF16)


def _in_proj_a(x, w_in, batch, seq):
    n = x.shape[0]
    w_in = w_in.astype(BF16)
    tiles_per_batch = seq // PROJ_TM
    chunks_per_tile = PROJ_TM // DIFF_TK
    est = D_MODEL * A_IN * 2 + 2 * PROJ_TM * D_MODEL * 4 + 2 * PROJ_TM * A_IN * 2 + 4 * PROJ_TM * PROJ_CHUNK * 4
    return pl.pallas_call(
        _in_proj_a_kernel,
        out_shape=(
            jax.ShapeDtypeStruct((n, 2 * QK_A), BF16),
            jax.ShapeDtypeStruct((batch, seq // DIFF_TK, TOK_WIDTH, DIFF_TK), BF16),
            jax.ShapeDtypeStruct((n, MEM_WIDTH), BF16),
        ),
        grid=(n // PROJ_TM,),
        in_specs=[
            pl.BlockSpec((PROJ_TM, D_MODEL), lambda i: (i, 0)),
            _resident((D_MODEL, A_IN), lambda i: (0, 0)),
        ],
        out_specs=(
            pl.BlockSpec((PROJ_TM, 2 * QK_A), lambda i: (i, 0)),
            pl.BlockSpec((None, chunks_per_tile, TOK_WIDTH, DIFF_TK),
                         lambda i: (i // tiles_per_batch, i % tiles_per_batch, 0, 0)),
            pl.BlockSpec((PROJ_TM, MEM_WIDTH), lambda i: (i, 0)),
        ),
        compiler_params=pltpu.CompilerParams(
            dimension_semantics=("arbitrary",), vmem_limit_bytes=_vmem_limit(est)),
        name="in_proj_a",
    )(x, w_in)


def _diff_attn_kernel(slopes_ref, q_ref, k_ref, vt_ref, lq1_ref, lk1_ref, lq2_ref, lk2_ref, g_ref,
                      o_ref, qt_ref, rel_ref, m_ref, l_ref, acc_ref, *, lam_init, seq):
    h = pl.program_id(1)
    i = pl.program_id(2)
    tq, tk = DIFF_TQ, DIFF_TK
    slope = slopes_ref[h]

    qt = q_ref[...].astype(F32).T
    row = lax.broadcasted_iota(jnp.int32, qt.shape, 0)
    zero = jnp.zeros_like(qt)
    qt_ref[:, :tq] = jnp.where(row < HEAD_DIM, qt, zero).astype(BF16)
    qt_ref[:, tq:] = jnp.where(row >= HEAD_DIM, qt, zero).astype(BF16)

    rel = (lax.broadcasted_iota(jnp.int32, (tk, tq), 0) - lax.broadcasted_iota(jnp.int32, (tk, tq), 1)).astype(F32)
    rel_ref[:, :tq] = rel
    rel_ref[:, tq:] = rel

    m_ref[...] = jnp.full(m_ref.shape, -jnp.inf, F32)
    l_ref[...] = jnp.zeros(l_ref.shape, F32)
    acc_ref[...] = jnp.zeros(acc_ref.shape, F32)

    def step(j, carry):
        kblk = k_ref[pl.ds(pl.multiple_of(j * tk, tk), tk), :]
        st = jnp.dot(kblk, qt_ref[...], preferred_element_type=F32)
        base = (j * tk - i * tq).astype(F32)
        st = st - slope * jnp.abs(rel_ref[...] + base)
        m_old = m_ref[...]
        m_new = jnp.maximum(m_old, jnp.max(st, axis=0, keepdims=True))
        alpha = jnp.exp(m_old - m_new)
        p = jnp.exp(st - m_new)
        l_ref[...] = alpha * l_ref[...] + jnp.sum(p, axis=0, keepdims=True)
        acc_ref[...] = alpha * acc_ref[...] + jnp.dot(vt_ref[j], p.astype(BF16), preferred_element_type=F32)
        m_ref[...] = m_new
        return carry

    lax.fori_loop(0, seq // tk, step, 0)

    lam = (jnp.exp(jnp.sum(lq1_ref[...] * lk1_ref[...], axis=-1, keepdims=True))
           - jnp.exp(jnp.sum(lq2_ref[...] * lk2_ref[...], axis=-1, keepdims=True)) + lam_init)
    o_t = acc_ref[...] / l_ref[...]
    a = (o_t[:, :tq] - lam * o_t[:, tq:]).T
    a = a * lax.rsqrt(jnp.mean(a * a, axis=-1, keepdims=True) + LN_EPS) * g_ref[...]
    o_ref[...] = (a * (1.0 - lam_init)).astype(BF16)


def _diff_attn(qk, vt, lq1, lk1, lq2, lk2, subln_g, lam_init, batch, seq):
    nq = seq // DIFF_TQ
    slopes = _alibi_slopes(DIFF_HEADS)
    vec = lambda v: v.reshape(1, -1).astype(F32)
    est = (2 * 2 * seq * DIFF_V_DIM * 2 + 4 * DIFF_TQ * DIFF_V_DIM * 2
           + 6 * DIFF_TK * 2 * DIFF_TQ * 4 + 2 * DIFF_V_DIM * 2 * DIFF_TQ * 4)
    small = lambda n: pl.BlockSpec((1, n), lambda b, h, i: (0, 0))
    return pl.pallas_call(
        functools.partial(_diff_attn_kernel, lam_init=lam_init, seq=seq),
        out_shape=jax.ShapeDtypeStruct((batch * seq, TOK_WIDTH), BF16),
        grid=(batch, DIFF_HEADS, nq),
        in_specs=[
            pl.BlockSpec(memory_space=pltpu.SMEM),
            pl.BlockSpec((DIFF_TQ, DIFF_V_DIM), lambda b, h, i: (b * nq + i, h)),
            pl.BlockSpec((seq, DIFF_V_DIM), lambda b, h, i: (b, DIFF_HEADS + h)),
            pl.BlockSpec((None, seq // DIFF_TK, DIFF_V_DIM, DIFF_TK), lambda b, h, i: (b, 0, h, 0)),
            small(HEAD_DIM), small(HEAD_DIM), small(HEAD_DIM), small(HEAD_DIM), small(DIFF_V_DIM),
        ],
        out_specs=pl.BlockSpec((DIFF_TQ, DIFF_V_DIM), lambda b, h, i: (b * nq + i, h)),
        scratch_shapes=[
            pltpu.VMEM((DIFF_V_DIM, 2 * DIFF_TQ), BF16),
            pltpu.VMEM((DIFF_TK, 2 * DIFF_TQ), F32),
            pltpu.VMEM((1, 2 * DIFF_TQ), F32),
            pltpu.VMEM((1, 2 * DIFF_TQ), F32),
            pltpu.VMEM((DIFF_V_DIM, 2 * DIFF_TQ), F32),
        ],
        compiler_params=pltpu.CompilerParams(
            dimension_semantics=("arbitrary", "arbitrary", "arbitrary"), vmem_limit_bytes=_vmem_limit(est)),
        name="diff_attn",
    )(slopes, qk, qk, vt, vec(lq1), vec(lk1), vec(lq2), vec(lk2), vec(subln_g))


def _mem_kv_kernel(mem_ref, w_ref, kt_ref, v_ref):
    kv = jnp.dot(mem_ref[...].astype(BF16), w_ref[...], preferred_element_type=F32)
    kt = kv[:, :MEM_WIDTH].T
    v = kv[:, MEM_WIDTH:]
    pair_rows = 2 * HEAD_DIM
    row = lax.broadcasted_iota(jnp.int32, (pair_rows, N_MEM), 0)
    col = lax.broadcasted_iota(jnp.int32, (N_MEM, MEM_WIDTH), 1)
    for h in range(MEM_HEADS):
        pair = kt[(h // 2) * pair_rows:(h // 2 + 1) * pair_rows, :]
        keep = (row >= (h % 2) * HEAD_DIM) & (row < (h % 2 + 1) * HEAD_DIM)
        kt_ref[h] = jnp.where(keep, pair, jnp.zeros_like(pair)).astype(BF16)
        keep_v = (col >= h * HEAD_DIM) & (col < (h + 1) * HEAD_DIM)
        v_ref[h] = jnp.where(keep_v, v, jnp.zeros_like(v)).astype(BF16)


def _mem_kv(mem, w_mem_kv):
    batch = mem.shape[0]
    return pl.pallas_call(
        _mem_kv_kernel,
        out_shape=(
            jax.ShapeDtypeStruct((batch, MEM_HEADS, 2 * HEAD_DIM, N_MEM), BF16),
            jax.ShapeDtypeStruct((batch, MEM_HEADS, N_MEM, MEM_WIDTH), BF16),
        ),
        grid=(batch,),
        in_specs=[
            pl.BlockSpec((None, N_MEM, D_MODEL), lambda b: (b, 0, 0)),
            pl.BlockSpec((D_MODEL, 2 * MEM_WIDTH), lambda b: (0, 0)),
        ],
        out_specs=(
            pl.BlockSpec((None, MEM_HEADS, 2 * HEAD_DIM, N_MEM), lambda b: (b, 0, 0, 0)),
            pl.BlockSpec((None, MEM_HEADS, N_MEM, MEM_WIDTH), lambda b: (b, 0, 0, 0)),
        ),
        name="mem_kv",
    )(mem, w_mem_kv.astype(BF16))


def _mix_out_kernel(x_ref, o_ref, qm_ref, kt_ref, v_ref, wo_ref, wm_ref, g_ref, b_ref, out_ref):
    qm = qm_ref[...]
    pair_w = 2 * HEAD_DIM
    mem_o = None
    for h in range(MEM_HEADS):
        q_pair = qm[:, (h // 2) * pair_w:(h // 2 + 1) * pair_w]
        s = jnp.dot(q_pair, kt_ref[h], preferred_element_type=F32)
        e = jnp.exp(s - jnp.max(s, axis=-1, keepdims=True))
        p = (e / jnp.sum(e, axis=-1, keepdims=True)).astype(BF16)
        part = jnp.dot(p, v_ref[h], preferred_element_type=F32)
        mem_o = part if mem_o is None else mem_o + part
    y = jnp.dot(o_ref[...], wo_ref[...], preferred_element_type=F32)
    y = y + jnp.dot(mem_o.astype(BF16), wm_ref[...], preferred_element_type=F32)
    out_ref[...] = _layer_norm(DN_ALPHA * x_ref[...] + y, g_ref[...], b_ref[...])


def _mix_out(x, o, qm_arr, qm_col, kt, v, w_out, g, b, seq):
    n = x.shape[0]
    tiles_per_batch = seq // MIX_TM
    w_out = w_out.astype(BF16)
    est = (D_MODEL * D_MODEL * 2 + 4 * MIX_TM * D_MODEL * 4 + 2 * MIX_TM * D_MODEL * 2
           + 4 * MEM_HEADS * N_MEM * MEM_WIDTH * 2 + 6 * MIX_TM * D_MODEL * 4)
    return pl.pallas_call(
        _mix_out_kernel,
        out_shape=jax.ShapeDtypeStruct((n, D_MODEL), F32),
        grid=(n // MIX_TM,),
        in_specs=[
            pl.BlockSpec((MIX_TM, D_MODEL), lambda i: (i, 0)),
            pl.BlockSpec((MIX_TM, TOK_WIDTH), lambda i: (i, 0)),
            pl.BlockSpec((MIX_TM, MEM_WIDTH), lambda i: (i, qm_col)),
            pl.BlockSpec((None, MEM_HEADS, 2 * HEAD_DIM, N_MEM), lambda i: (i // tiles_per_batch, 0, 0, 0)),
            pl.BlockSpec((None, MEM_HEADS, N_MEM, MEM_WIDTH), lambda i: (i // tiles_per_batch, 0, 0, 0)),
            _resident((TOK_WIDTH, D_MODEL), lambda i: (0, 0)),
            _resident((MEM_WIDTH, D_MODEL), lambda i: (TOK_WIDTH // MEM_WIDTH, 0)),
            _resident((1, D_MODEL), lambda i: (0, 0)),
            _resident((1, D_MODEL), lambda i: (0, 0)),
        ],
        out_specs=pl.BlockSpec((MIX_TM, D_MODEL), lambda i: (i, 0)),
        compiler_params=pltpu.CompilerParams(
            dimension_semantics=("arbitrary",), vmem_limit_bytes=_vmem_limit(est)),
        name="mix_out",
    )(x, o, qm_arr, kt, v, w_out, w_out, g.reshape(1, D_MODEL), b.reshape(1, D_MODEL))


def _in_proj_b_kernel(x_ref, w_ref, h_ref):
    xb = x_ref[...].astype(BF16)
    k_lo = TOK_WIDTH // PROJ_CHUNK
    k_hi = (TOK_WIDTH + 2 * WIN_KV_WIDTH) // PROJ_CHUNK
    for c in range(B_IN // PROJ_CHUNK):
        cols = slice(c * PROJ_CHUNK, (c + 1) * PROJ_CHUNK)
        r = jnp.dot(xb, w_ref[:, cols], preferred_element_type=F32)
        if c < k_lo or c >= k_hi:
            r = r * QK_SCALE
        h_ref[:, cols] = r.astype(BF16)


def _in_proj_b(x, w_in):
    n = x.shape[0]
    est = D_MODEL * B_IN * 2 + 2 * PROJ_TM * D_MODEL * 4 + 2 * PROJ_TM * B_IN * 2 + 4 * PROJ_TM * PROJ_CHUNK * 4
    return pl.pallas_call(
        _in_proj_b_kernel,
        out_shape=jax.ShapeDtypeStruct((n, B_IN), BF16),
        grid=(n // PROJ_TM,),
        in_specs=[
            pl.BlockSpec((PROJ_TM, D_MODEL), lambda i: (i, 0)),
            _resident((D_MODEL, B_IN), lambda i: (0, 0)),
        ],
        out_specs=pl.BlockSpec((PROJ_TM, B_IN), lambda i: (i, 0)),
        compiler_params=pltpu.CompilerParams(
            dimension_semantics=("arbitrary",), vmem_limit_bytes=_vmem_limit(est)),
        name="in_proj_b",
    )(x, w_in.astype(BF16))


def _win_attn_kernel(slopes_ref, sink_ref, q_ref, k_ref, v_ref, o_ref, *, seq):
    i = pl.program_id(1)
    span = 3 * WINDOW
    blocks = WIN_TQ // WINDOW
    rel = (lax.broadcasted_iota(jnp.int32, (WINDOW, span), 0)
           - lax.broadcasted_iota(jnp.int32, (WINDOW, span), 1))

    def block(nn, carry):
        t0 = (i * blocks + nn) * WINDOW
        start = pl.multiple_of(jnp.clip(t0 - WINDOW, 0, seq - span), WINDOW)
        dist = jnp.abs(rel + (t0 - start))
        valid = dist <= WINDOW
        dist = dist.astype(F32)
        kwin = k_ref[pl.ds(start, span), :]
        vwin = v_ref[pl.ds(start, span), :]
        rows = pl.ds(pl.multiple_of(nn * WINDOW, WINDOW), WINDOW)
        outs = []
        for h in range(WIN_Q_HEADS):
            kv = h // WIN_GROUP
            qh = q_ref[rows, h * HEAD_DIM:(h + 1) * HEAD_DIM]
            kh = kwin[:, kv * HEAD_DIM:(kv + 1) * HEAD_DIM]
            s = lax.dot_general(qh, kh, (((1,), (1,)), ((), ())), preferred_element_type=F32)
            s = jnp.where(valid, s - slopes_ref[h] * dist, NEG)
            sink = sink_ref[h]
            m = jnp.maximum(jnp.max(s, axis=-1, keepdims=True), sink)
            e = jnp.exp(s - m)
            denom = jnp.sum(e, axis=-1, keepdims=True) + jnp.exp(sink - m)
            p = (e / denom).astype(BF16)
            outs.append(jnp.dot(p, vwin[:, kv * HEAD_DIM:(kv + 1) * HEAD_DIM], preferred_element_type=F32))
        o_ref[rows, :] = jnp.concatenate(outs, axis=-1).astype(BF16)
        return carry

    lax.fori_loop(0, blocks, block, 0)


def _win_attn(h, sink, batch, seq):
    nq = seq // WIN_TQ
    slopes = _alibi_slopes(WIN_Q_HEADS)
    est = 2 * 2 * seq * WIN_KV_WIDTH * 2 + 4 * WIN_TQ * TOK_WIDTH * 2 + 16 * WINDOW * 3 * WINDOW * 4
    return pl.pallas_call(
        functools.partial(_win_attn_kernel, seq=seq),
        out_shape=jax.ShapeDtypeStruct((batch * seq, TOK_WIDTH), BF16),
        grid=(batch, nq),
        in_specs=[
            pl.BlockSpec(memory_space=pltpu.SMEM),
            pl.BlockSpec(memory_space=pltpu.SMEM),
            pl.BlockSpec((WIN_TQ, TOK_WIDTH), lambda b, i: (b * nq + i, 0)),
            pl.BlockSpec((seq, WIN_KV_WIDTH), lambda b, i: (b, TOK_WIDTH // WIN_KV_WIDTH)),
            pl.BlockSpec((seq, WIN_KV_WIDTH), lambda b, i: (b, TOK_WIDTH // WIN_KV_WIDTH + 1)),
        ],
        out_specs=pl.BlockSpec((WIN_TQ, TOK_WIDTH), lambda b, i: (b * nq + i, 0)),
        compiler_params=pltpu.CompilerParams(
            dimension_semantics=("arbitrary", "arbitrary"), vmem_limit_bytes=_vmem_limit(est)),
        name="win_attn",
    )(slopes, sink.astype(F32), h, h, h)


def kernel(x, mem, ffn1_w13, ffn1_w2, ln1_g, ln1_b, w_mem_kv, w_out, ln2_g, ln2_b, ffn2_w13, ffn2_w2,
           ln3_g, ln3_b, a_w_in, a_lambda_q1, a_lambda_k1, a_lambda_q2, a_lambda_k2, a_subln_g, b_w_in, b_sink):
    batch, seq, _ = x.shape
    xf = x.reshape(batch * seq, D_MODEL)
    for i in range(DEPTH):
        xf = _ffn_ln(xf, ffn1_w13[i], ffn1_w2[i], ln1_g[i], ln1_b[i])
        j = i // N_MIXERS
        kt, v = _mem_kv(mem, w_mem_kv[i])
        if i % N_MIXERS == 0:
            lam_init = 0.8 - 0.6 * math.exp(-0.3 * i)
            qk, vt, qm = _in_proj_a(xf, a_w_in[j], batch, seq)
            o = _diff_attn(qk, vt, a_lambda_q1[j], a_lambda_k1[j], a_lambda_q2[j], a_lambda_k2[j],
                           a_subln_g[j], lam_init, batch, seq)
            xf = _mix_out(xf, o, qm, 0, kt, v, w_out[i], ln2_g[i], ln2_b[i], seq)
        else:
            h = _in_proj_b(xf, b_w_in[j])
            o = _win_attn(h, b_sink[j], batch, seq)
            xf = _mix_out(xf, o, h, (TOK_WIDTH + 2 * WIN_KV_WIDTH) // MEM_WIDTH, kt, v, w_out[i],
                          ln2_g[i], ln2_b[i], seq)
        xf = _ffn_ln(xf, ffn2_w13[i], ffn2_w2[i], ln3_g[i], ln3_b[i])
    return xf.reshape(batch, seq, D_MODEL)
```

```python
import functools
import math

import numpy as np
import jax
import jax.numpy as jnp
from jax import lax
from jax.experimental import pallas as pl
from jax.experimental.pallas import tpu as pltpu

D_MODEL = 1024
DEPTH = 2
N_MIXERS = 2
N_MEM = 256
HEAD_DIM = 64
MEM_HEADS = 4
MEM_WIDTH = MEM_HEADS * HEAD_DIM
TOK_WIDTH = D_MODEL - MEM_WIDTH
DIFF_HEADS = TOK_WIDTH // (2 * HEAD_DIM)
DIFF_V_DIM = 2 * HEAD_DIM
QK_A = DIFF_HEADS * 2 * HEAD_DIM
WIN_Q_HEADS = TOK_WIDTH // HEAD_DIM
WIN_GROUP = 3
WIN_KV_HEADS = WIN_Q_HEADS // WIN_GROUP
WIN_KV_WIDTH = WIN_KV_HEADS * HEAD_DIM
WINDOW = 128
D_FF = 2816
A_IN = 2 * QK_A + TOK_WIDTH + MEM_WIDTH
B_IN = TOK_WIDTH + 2 * WIN_KV_WIDTH + MEM_WIDTH
N_A = (DEPTH + 1) // 2
DN_ALPHA = (2 * DEPTH) ** 0.25
LN_EPS = 1e-5
NEG = -1e30
QK_SCALE = HEAD_DIM ** -0.5

F32 = jnp.float32
BF16 = jnp.bfloat16

V7X_LANES = 128
V7X_MXU_DIM = 256
V7X_VMEM_BYTES = 64 * 1024 * 1024

FFN_TM = 512
FF_CHUNK = V7X_MXU_DIM
PROJ_TM = 1024
PROJ_CHUNK = V7X_MXU_DIM
DIFF_TQ = 256
DIFF_TK = 256
MIX_TM = 512
WIN_TQ = 512


def _vmem_limit(nbytes):
    return int(min(V7X_VMEM_BYTES - 8 * 1024 * 1024, nbytes * 3 // 2))


def _resident(block_shape, index_map):
    return pl.BlockSpec(block_shape, index_map, pipeline_mode=pl.Buffered(1))


def _layer_norm(y, g, b):
    mu = jnp.mean(y, axis=-1, keepdims=True)
    yc = y - mu
    var = jnp.mean(yc * yc, axis=-1, keepdims=True)
    return yc * lax.rsqrt(var + LN_EPS) * g + b


def _alibi_slopes(n):
    return jnp.asarray(2.0 ** (-8.0 * np.arange(1, n + 1) / n), dtype=F32)


def _ffn_ln_kernel(x_ref, wg_ref, wu_ref, w2_ref, g_ref, b_ref, o_ref, acc_ref):
    x = x_ref[...]
    xb = x.astype(BF16)
    for c in range(D_FF // FF_CHUNK):
        cols = slice(c * FF_CHUNK, (c + 1) * FF_CHUNK)
        gate = jnp.dot(xb, wg_ref[:, cols], preferred_element_type=F32)
        up = jnp.dot(xb, wu_ref[:, cols], preferred_element_type=F32)
        act = (gate / (1.0 + jnp.exp(-gate)) * up).astype(BF16)
        part = jnp.dot(act, w2_ref[cols, :], preferred_element_type=F32)
        if c == 0:
            acc_ref[...] = part
        else:
            acc_ref[...] += part
    y = DN_ALPHA * x + 0.5 * acc_ref[...]
    o_ref[...] = _layer_norm(y, g_ref[...], b_ref[...])


def _ffn_ln(x, w13, w2, g, b):
    n = x.shape[0]
    w13 = w13.astype(BF16)
    w2 = w2.astype(BF16)
    est = (2 * D_MODEL * D_FF + D_FF * D_MODEL) * 2 + 5 * FFN_TM * D_MODEL * 4 + 4 * FFN_TM * FF_CHUNK * 4
    return pl.pallas_call(
        _ffn_ln_kernel,
        out_shape=jax.ShapeDtypeStruct((n, D_MODEL), F32),
        grid=(n // FFN_TM,),
        in_specs=[
            pl.BlockSpec((FFN_TM, D_MODEL), lambda i: (i, 0)),
            _resident((D_MODEL, D_FF), lambda i: (0, 0)),
            _resident((D_MODEL, D_FF), lambda i: (0, 1)),
            _resident((D_FF, D_MODEL), lambda i: (0, 0)),
            _resident((1, D_MODEL), lambda i: (0, 0)),
            _resident((1, D_MODEL), lambda i: (0, 0)),
        ],
        out_specs=pl.BlockSpec((FFN_TM, D_MODEL), lambda i: (i, 0)),
        scratch_shapes=[pltpu.VMEM((FFN_TM, D_MODEL), F32)],
        compiler_params=pltpu.CompilerParams(
            dimension_semantics=("arbitrary",), vmem_limit_bytes=_vmem_limit(est)),
        name="ffn_ln",
    )(x, w13, w13, w2, g.reshape(1, D_MODEL), b.reshape(1, D_MODEL))


def _in_proj_a_kernel(x_ref, w_ref, qk_ref, vt_ref, qm_ref):
    xb = x_ref[...].astype(BF16)
    n_qk = 2 * QK_A // PROJ_CHUNK
    n_v = TOK_WIDTH // PROJ_CHUNK
    for c in range(A_IN // PROJ_CHUNK):
        cols = slice(c * PROJ_CHUNK, (c + 1) * PROJ_CHUNK)
        r = jnp.dot(xb, w_ref[:, cols], preferred_element_type=F32)
        if c < QK_A // PROJ_CHUNK:
            qk_ref[:, cols] = (r * QK_SCALE).astype(BF16)
        elif c < n_qk:
            qk_ref[:, cols] = r.astype(BF16)
        elif c < n_qk + n_v:
            rt = r.T.astype(BF16)
            rows = slice((c - n_qk) * PROJ_CHUNK, (c - n_qk + 1) * PROJ_CHUNK)
            for t in range(PROJ_TM // DIFF_TK):
                vt_ref[t, rows, :] = rt[:, t * DIFF_TK:(t + 1) * DIFF_TK]
        else:
            qm_ref[...] = (r * QK_SCALE).astype(BF16)


def _in_proj_a(x, w_in, batch, seq):
    n = x.shape[0]
    w_in = w_in.astype(BF16)
    tiles_per_batch = seq // PROJ_TM
    chunks_per_tile = PROJ_TM // DIFF_TK
    est = D_MODEL * A_IN * 2 + 2 * PROJ_TM * D_MODEL * 4 + 2 * PROJ_TM * A_IN * 2 + 4 * PROJ_TM * PROJ_CHUNK * 4
    return pl.pallas_call(
        _in_proj_a_kernel,
        out_shape=(
            jax.ShapeDtypeStruct((n, 2 * QK_A), BF16),
            jax.ShapeDtypeStruct((batch, seq // DIFF_TK, TOK_WIDTH, DIFF_TK), BF16),
            jax.ShapeDtypeStruct((n, MEM_WIDTH), BF16),
        ),
        grid=(n // PROJ_TM,),
        in_specs=[
            pl.BlockSpec((PROJ_TM, D_MODEL), lambda i: (i, 0)),
            _resident((D_MODEL, A_IN), lambda i: (0, 0)),
        ],
        out_specs=(
            pl.BlockSpec((PROJ_TM, 2 * QK_A), lambda i: (i, 0)),
            pl.BlockSpec((None, chunks_per_tile, TOK_WIDTH, DIFF_TK),
                         lambda i: (i // tiles_per_batch, i % tiles_per_batch, 0, 0)),
            pl.BlockSpec((PROJ_TM, MEM_WIDTH), lambda i: (i, 0)),
        ),
        compiler_params=pltpu.CompilerParams(
            dimension_semantics=("arbitrary",), vmem_limit_bytes=_vmem_limit(est)),
        name="in_proj_a",
    )(x, w_in)


def _diff_attn_kernel(slopes_ref, q_ref, k_ref, vt_ref, lq1_ref, lk1_ref, lq2_ref, lk2_ref, g_ref,
                      o_ref, qt_ref, pos_ref, bias_ref, sr_ref, m_ref, l_ref, acc_ref,
                      st_a, st_b, shift_a, shift_b, p_a, p_b, alpha_a, alpha_b, *, lam_init, seq):
    h = pl.program_id(1)
    i = pl.program_id(2)
    tq, tk = DIFF_TQ, DIFF_TK
    assert tq == tk and tk <= 256
    n_steps = seq // tk
    n_strips = 2 * tq // V7X_LANES
    slope = slopes_ref[h]

    qt = q_ref[...].astype(F32).T
    row = lax.broadcasted_iota(jnp.int32, qt.shape, 0)
    zero = jnp.zeros_like(qt)
    qt_ref[0:DIFF_V_DIM, :tq] = jnp.where(row < HEAD_DIM, qt, zero).astype(BF16)
    qt_ref[0:DIFF_V_DIM, tq:] = jnp.where(row >= HEAD_DIM, qt, zero).astype(BF16)
    slope_v = jnp.full((16, 2 * tq), slope, F32)
    s_hi = slope_v.astype(BF16).astype(F32)
    s_mid = (slope_v - s_hi).astype(BF16).astype(F32)
    s_lo = slope_v - s_hi - s_mid
    prow = lax.broadcasted_iota(jnp.int32, (16, 2 * tq), 0)
    pieces = jnp.where(prow == 0, s_hi, jnp.where(prow == 1, s_mid, jnp.where(prow == 2, s_lo, 0.0)))
    qt_ref[DIFF_V_DIM:DIFF_V_DIM + 16, :] = pieces.astype(BF16)
    qt_ref[DIFF_V_DIM + 16:, :] = jnp.zeros((DIFF_V_DIM - 16, 2 * tq), BF16)

    r_s = lax.broadcasted_iota(jnp.int32, (tk, V7X_LANES), 0).astype(F32)
    lane = lax.broadcasted_iota(jnp.int32, (tk, V7X_LANES), 1)
    r_s = jnp.where(lane < 3, r_s, 0.0)
    pos_ref[0] = r_s.astype(BF16)
    pos_ref[1] = (-r_s).astype(BF16)

    rel = (lax.broadcasted_iota(jnp.int32, (tk, tq), 0) - lax.broadcasted_iota(jnp.int32, (tk, tq), 1))
    bias_ref[...] = slope * jnp.abs(rel).astype(F32)
    r_t = lax.broadcasted_iota(jnp.int32, (1, 2 * tq), 1)
    sr_ref[...] = slope * jnp.where(r_t >= tq, r_t - tq, r_t).astype(F32)

    m_ref[...] = jnp.full(m_ref.shape, -jnp.inf, F32)
    l_ref[...] = jnp.zeros(l_ref.shape, F32)
    acc_ref[...] = jnp.zeros(acc_ref.shape, F32)

    def keys(j):
        return k_ref[pl.ds(pl.multiple_of(j * tk, tk), tk), :]

    def tile_of(t):
        return jnp.where(t > i, t, t - 1)

    def scores_diag(st_ref, shift_ref):
        st = jnp.dot(keys(i), qt_ref[0:DIFF_V_DIM, :], preferred_element_type=F32)
        bias = bias_ref[...]
        st_ref[...] = st - jnp.concatenate([bias, bias], axis=1)
        shift_ref[...] = jnp.zeros((1, 2 * tq), F32)

    def scores_off(t, st_ref, shift_ref):
        j = tile_of(t)
        after = (j > i).astype(jnp.int32)
        k_aug = jnp.concatenate([keys(j), pos_ref[after]], axis=1)
        st_ref[...] = jnp.dot(k_aug, qt_ref[...], preferred_element_type=F32)
        sign = (1 - 2 * after).astype(F32)
        tile_dist = (jnp.abs(j - i) * tk).astype(F32)
        shift_ref[...] = -(sign * sr_ref[...]) - slope * tile_dist

    def softmax(st_ref, shift_ref, p_ref, alpha_ref):
        for c in range(n_strips):
            lanes = slice(c * V7X_LANES, (c + 1) * V7X_LANES)
            s = st_ref[:, lanes]
            shift = shift_ref[:, lanes]
            m_old = m_ref[:, lanes]
            m_new = jnp.maximum(m_old, jnp.max(s, axis=0, keepdims=True) + shift)
            alpha = jnp.exp(m_old - m_new)
            p = jnp.exp(s - (m_new - shift))
            l_ref[:, lanes] = alpha * l_ref[:, lanes] + jnp.sum(p, axis=0, keepdims=True)
            m_ref[:, lanes] = m_new
            p_ref[:, lanes] = p.astype(BF16)
            alpha_ref[:, lanes] = alpha

    def pv(j, p_ref, alpha_ref):
        acc_ref[...] = alpha_ref[...] * acc_ref[...] + jnp.dot(vt_ref[j], p_ref[...],
                                                                preferred_element_type=F32)

    scores_diag(st_a, shift_a)
    scores_off(1, st_b, shift_b)
    softmax(st_a, shift_a, p_a, alpha_a)

    def body(k, carry):
        t = 1 + 2 * k
        scores_off(t + 1, st_a, shift_a)
        softmax(st_b, shift_b, p_b, alpha_b)
        pv(jnp.where(k == 0, i, tile_of(t - 1)), p_a, alpha_a)
        scores_off(t + 2, st_b, shift_b)
        softmax(st_a, shift_a, p_a, alpha_a)
        pv(tile_of(t), p_b, alpha_b)
        return carry

    assert n_steps % 2 == 0
    lax.fori_loop(0, (n_steps - 2) // 2, body, 0)
    softmax(st_b, shift_b, p_b, alpha_b)
    pv(tile_of(n_steps - 2), p_a, alpha_a)
    pv(tile_of(n_steps - 1), p_b, alpha_b)

    lam = (jnp.exp(jnp.sum(lq1_ref[...] * lk1_ref[...], axis=-1, keepdims=True))
           - jnp.exp(jnp.sum(lq2_ref[...] * lk2_ref[...], axis=-1, keepdims=True)) + lam_init)
    o_t = acc_ref[...] / l_ref[...]
    a = (o_t[:, :tq] - lam * o_t[:, tq:]).T
    a = a * lax.rsqrt(jnp.mean(a * a, axis=-1, keepdims=True) + LN_EPS) * g_ref[...]
    o_ref[...] = (a * (1.0 - lam_init)).astype(BF16)


def _diff_attn(qk, vt, lq1, lk1, lq2, lk2, subln_g, lam_init, batch, seq):
    nq = seq // DIFF_TQ
    slopes = _alibi_slopes(DIFF_HEADS)
    vec = lambda v: v.reshape(1, -1).astype(F32)
    est = (2 * 2 * seq * DIFF_V_DIM * 2 + 4 * DIFF_TQ * DIFF_V_DIM * 2
           + 6 * DIFF_TK * 2 * DIFF_TQ * 4 + 2 * DIFF_V_DIM * 2 * DIFF_TQ * 4)
    small = lambda n: pl.BlockSpec((1, n), lambda b, h, i: (0, 0))
    return pl.pallas_call(
        functools.partial(_diff_attn_kernel, lam_init=lam_init, seq=seq),
        out_shape=jax.ShapeDtypeStruct((batch * seq, TOK_WIDTH), BF16),
        grid=(batch, DIFF_HEADS, nq),
        in_specs=[
            pl.BlockSpec(memory_space=pltpu.SMEM),
            pl.BlockSpec((DIFF_TQ, DIFF_V_DIM), lambda b, h, i: (b * nq + i, h)),
            pl.BlockSpec((seq, DIFF_V_DIM), lambda b, h, i: (b, DIFF_HEADS + h)),
            pl.BlockSpec((None, seq // DIFF_TK, DIFF_V_DIM, DIFF_TK), lambda b, h, i: (b, 0, h, 0)),
            small(HEAD_DIM), small(HEAD_DIM), small(HEAD_DIM), small(HEAD_DIM), small(DIFF_V_DIM),
        ],
        out_specs=pl.BlockSpec((DIFF_TQ, DIFF_V_DIM), lambda b, h, i: (b * nq + i, h)),
        scratch_shapes=[
            pltpu.VMEM((2 * DIFF_V_DIM, 2 * DIFF_TQ), BF16),
            pltpu.VMEM((2, DIFF_TK, V7X_LANES), BF16),
            pltpu.VMEM((DIFF_TK, DIFF_TQ), F32),
            pltpu.VMEM((1, 2 * DIFF_TQ), F32),
            pltpu.VMEM((1, 2 * DIFF_TQ), F32),
            pltpu.VMEM((1, 2 * DIFF_TQ), F32),
            pltpu.VMEM((DIFF_V_DIM, 2 * DIFF_TQ), F32),
            pltpu.VMEM((DIFF_TK, 2 * DIFF_TQ), F32), pltpu.VMEM((DIFF_TK, 2 * DIFF_TQ), F32),
            pltpu.VMEM((1, 2 * DIFF_TQ), F32), pltpu.VMEM((1, 2 * DIFF_TQ), F32),
            pltpu.VMEM((DIFF_TK, 2 * DIFF_TQ), BF16), pltpu.VMEM((DIFF_TK, 2 * DIFF_TQ), BF16),
            pltpu.VMEM((1, 2 * DIFF_TQ), F32), pltpu.VMEM((1, 2 * DIFF_TQ), F32),
        ],
        compiler_params=pltpu.CompilerParams(
            dimension_semantics=("arbitrary", "arbitrary", "arbitrary"), vmem_limit_bytes=_vmem_limit(est)),
        name="diff_attn",
    )(slopes, qk, qk, vt, vec(lq1), vec(lk1), vec(lq2), vec(lk2), vec(subln_g))


def _mem_kv_kernel(mem_ref, w_ref, kt_ref, v_ref):
    kv = jnp.dot(mem_ref[...].astype(BF16), w_ref[...], preferred_element_type=F32)
    kt = kv[:, :MEM_WIDTH].T
    v = kv[:, MEM_WIDTH:]
    pair_rows = 2 * HEAD_DIM
    row = lax.broadcasted_iota(jnp.int32, (pair_rows, N_MEM), 0)
    col = lax.broadcasted_iota(jnp.int32, (N_MEM, MEM_WIDTH), 1)
    for h in range(MEM_HEADS):
        pair = kt[(h // 2) * pair_rows:(h // 2 + 1) * pair_rows, :]
        keep = (row >= (h % 2) * HEAD_DIM) & (row < (h % 2 + 1) * HEAD_DIM)
        kt_ref[h] = jnp.where(keep, pair, jnp.zeros_like(pair)).astype(BF16)
        keep_v = (col >= h * HEAD_DIM) & (col < (h + 1) * HEAD_DIM)
        v_ref[h] = jnp.where(keep_v, v, jnp.zeros_like(v)).astype(BF16)


def _mem_kv(mem, w_mem_kv):
    batch = mem.shape[0]
    return pl.pallas_call(
        _mem_kv_kernel,
        out_shape=(
            jax.ShapeDtypeStruct((batch, MEM_HEADS, 2 * HEAD_DIM, N_MEM), BF16),
            jax.ShapeDtypeStruct((batch, MEM_HEADS, N_MEM, MEM_WIDTH), BF16),
        ),
        grid=(batch,),
        in_specs=[
            pl.BlockSpec((None, N_MEM, D_MODEL), lambda b: (b, 0, 0)),
            pl.BlockSpec((D_MODEL, 2 * MEM_WIDTH), lambda b: (0, 0)),
        ],
        out_specs=(
            pl.BlockSpec((None, MEM_HEADS, 2 * HEAD_DIM, N_MEM), lambda b: (b, 0, 0, 0)),
            pl.BlockSpec((None, MEM_HEADS, N_MEM, MEM_WIDTH), lambda b: (b, 0, 0, 0)),
        ),
        name="mem_kv",
    )(mem, w_mem_kv.astype(BF16))


def _mix_out_kernel(x_ref, o_ref, qm_ref, kt_ref, v_ref, wo_ref, wm_ref, g_ref, b_ref, out_ref):
    qm = qm_ref[...]
    pair_w = 2 * HEAD_DIM
    mem_o = None
    for h in range(MEM_HEADS):
        q_pair = qm[:, (h // 2) * pair_w:(h // 2 + 1) * pair_w]
        s = jnp.dot(q_pair, kt_ref[h], preferred_element_type=F32)
        e = jnp.exp(s - jnp.max(s, axis=-1, keepdims=True))
        p = (e / jnp.sum(e, axis=-1, keepdims=True)).astype(BF16)
        part = jnp.dot(p, v_ref[h], preferred_element_type=F32)
        mem_o = part if mem_o is None else mem_o + part
    y = jnp.dot(o_ref[...], wo_ref[...], preferred_element_type=F32)
    y = y + jnp.dot(mem_o.astype(BF16), wm_ref[...], preferred_element_type=F32)
    out_ref[...] = _layer_norm(DN_ALPHA * x_ref[...] + y, g_ref[...], b_ref[...])


def _mix_out(x, o, qm_arr, qm_col, kt, v, w_out, g, b, seq):
    n = x.shape[0]
    tiles_per_batch = seq // MIX_TM
    w_out = w_out.astype(BF16)
    est = (D_MODEL * D_MODEL * 2 + 4 * MIX_TM * D_MODEL * 4 + 2 * MIX_TM * D_MODEL * 2
           + 4 * MEM_HEADS * N_MEM * MEM_WIDTH * 2 + 6 * MIX_TM * D_MODEL * 4)
    return pl.pallas_call(
        _mix_out_kernel,
        out_shape=jax.ShapeDtypeStruct((n, D_MODEL), F32),
        grid=(n // MIX_TM,),
        in_specs=[
            pl.BlockSpec((MIX_TM, D_MODEL), lambda i: (i, 0)),
            pl.BlockSpec((MIX_TM, TOK_WIDTH), lambda i: (i, 0)),
            pl.BlockSpec((MIX_TM, MEM_WIDTH), lambda i: (i, qm_col)),
            pl.BlockSpec((None, MEM_HEADS, 2 * HEAD_DIM, N_MEM), lambda i: (i // tiles_per_batch, 0, 0, 0)),
            pl.BlockSpec((None, MEM_HEADS, N_MEM, MEM_WIDTH), lambda i: (i // tiles_per_batch, 0, 0, 0)),
            _resident((TOK_WIDTH, D_MODEL), lambda i: (0, 0)),
            _resident((MEM_WIDTH, D_MODEL), lambda i: (TOK_WIDTH // MEM_WIDTH, 0)),
            _resident((1, D_MODEL), lambda i: (0, 0)),
            _resident((1, D_MODEL), lambda i: (0, 0)),
        ],
        out_specs=pl.BlockSpec((MIX_TM, D_MODEL), lambda i: (i, 0)),
        compiler_params=pltpu.CompilerParams(
            dimension_semantics=("arbitrary",), vmem_limit_bytes=_vmem_limit(est)),
        name="mix_out",
    )(x, o, qm_arr, kt, v, w_out, w_out, g.reshape(1, D_MODEL), b.reshape(1, D_MODEL))


def _in_proj_b_kernel(x_ref, w_ref, h_ref):
    xb = x_ref[...].astype(BF16)
    k_lo = TOK_WIDTH // PROJ_CHUNK
    k_hi = (TOK_WIDTH + 2 * WIN_KV_WIDTH) // PROJ_CHUNK
    for c in range(B_IN // PROJ_CHUNK):
        cols = slice(c * PROJ_CHUNK, (c + 1) * PROJ_CHUNK)
        r = jnp.dot(xb, w_ref[:, cols], preferred_element_type=F32)
        if c < k_lo or c >= k_hi:
            r = r * QK_SCALE
        h_ref[:, cols] = r.astype(BF16)


def _in_proj_b(x, w_in):
    n = x.shape[0]
    est = D_MODEL * B_IN * 2 + 2 * PROJ_TM * D_MODEL * 4 + 2 * PROJ_TM * B_IN * 2 + 4 * PROJ_TM * PROJ_CHUNK * 4
    return pl.pallas_call(
        _in_proj_b_kernel,
        out_shape=jax.ShapeDtypeStruct((n, B_IN), BF16),
        grid=(n // PROJ_TM,),
        in_specs=[
            pl.BlockSpec((PROJ_TM, D_MODEL), lambda i: (i, 0)),
            _resident((D_MODEL, B_IN), lambda i: (0, 0)),
        ],
        out_specs=pl.BlockSpec((PROJ_TM, B_IN), lambda i: (i, 0)),
        compiler_params=pltpu.CompilerParams(
            dimension_semantics=("arbitrary",), vmem_limit_bytes=_vmem_limit(est)),
        name="in_proj_b",
    )(x, w_in.astype(BF16))


def _win_attn_kernel(slopes_ref, sink_ref, q_ref, k_ref, v_ref, o_ref, *, seq):
    i = pl.program_id(1)
    span = 3 * WINDOW
    blocks = WIN_TQ // WINDOW
    rel = (lax.broadcasted_iota(jnp.int32, (WINDOW, span), 0)
           - lax.broadcasted_iota(jnp.int32, (WINDOW, span), 1))

    def block(nn, carry):
        t0 = (i * blocks + nn) * WINDOW
        start = pl.multiple_of(jnp.clip(t0 - WINDOW, 0, seq - span), WINDOW)
        dist = jnp.abs(rel + (t0 - start))
        valid = dist <= WINDOW
        dist = dist.astype(F32)
        kwin = k_ref[pl.ds(start, span), :]
        vwin = v_ref[pl.ds(start, span), :]
        rows = pl.ds(pl.multiple_of(nn * WINDOW, WINDOW), WINDOW)
        outs = []
        for h in range(WIN_Q_HEADS):
            kv = h // WIN_GROUP
            qh = q_ref[rows, h * HEAD_DIM:(h + 1) * HEAD_DIM]
            kh = kwin[:, kv * HEAD_DIM:(kv + 1) * HEAD_DIM]
            s = lax.dot_general(qh, kh, (((1,), (1,)), ((), ())), preferred_element_type=F32)
            s = jnp.where(valid, s - slopes_ref[h] * dist, NEG)
            sink = sink_ref[h]
            m = jnp.maximum(jnp.max(s, axis=-1, keepdims=True), sink)
            e = jnp.exp(s - m)
            denom = jnp.sum(e, axis=-1, keepdims=True) + jnp.exp(sink - m)
            p = (e / denom).astype(BF16)
            outs.append(jnp.dot(p, vwin[:, kv * HEAD_DIM:(kv + 1) * HEAD_DIM], preferred_element_type=F32))
        o_ref[rows, :] = jnp.concatenate(outs, axis=-1).astype(BF16)
        return carry

    lax.fori_loop(0, blocks, block, 0)


def _win_attn(h, sink, batch, seq):
    nq = seq // WIN_TQ
    slopes = _alibi_slopes(WIN_Q_HEADS)
    est = 2 * 2 * seq * WIN_KV_WIDTH * 2 + 4 * WIN_TQ * TOK_WIDTH * 2 + 16 * WINDOW * 3 * WINDOW * 4
    return pl.pallas_call(
        functools.partial(_win_attn_kernel, seq=seq),
        out_shape=jax.ShapeDtypeStruct((batch * seq, TOK_WIDTH), BF16),
        grid=(batch, nq),
        in_specs=[
            pl.BlockSpec(memory_space=pltpu.SMEM),
            pl.BlockSpec(memory_space=pltpu.SMEM),
            pl.BlockSpec((WIN_TQ, TOK_WIDTH), lambda b, i: (b * nq + i, 0)),
            pl.BlockSpec((seq, WIN_KV_WIDTH), lambda b, i: (b, TOK_WIDTH // WIN_KV_WIDTH)),
            pl.BlockSpec((seq, WIN_KV_WIDTH), lambda b, i: (b, TOK_WIDTH // WIN_KV_WIDTH + 1)),
        ],
        out_specs=pl.BlockSpec((WIN_TQ, TOK_WIDTH), lambda b, i: (b * nq + i, 0)),
        compiler_params=pltpu.CompilerParams(
            dimension_semantics=("arbitrary", "arbitrary"), vmem_limit_bytes=_vmem_limit(est)),
        name="win_attn",
    )(slopes, sink.astype(F32), h, h, h)


def kernel(x, mem, ffn1_w13, ffn1_w2, ln1_g, ln1_b, w_mem_kv, w_out, ln2_g, ln2_b, ffn2_w13, ffn2_w2,
           ln3_g, ln3_b, a_w_in, a_lambda_q1, a_lambda_k1, a_lambda_q2, a_lambda_k2, a_subln_g, b_w_in, b_sink):
    batch, seq, _ = x.shape
    xf = x.reshape(batch * seq, D_MODEL)
    for i in range(DEPTH):
        xf = _ffn_ln(xf, ffn1_w13[i], ffn1_w2[i], ln1_g[i], ln1_b[i])
        j = i // N_MIXERS
        kt, v = _mem_kv(mem, w_mem_kv[i])
        if i % N_MIXERS == 0:
            lam_init = 0.8 - 0.6 * math.exp(-0.3 * i)
            qk, vt, qm = _in_proj_a(xf, a_w_in[j], batch, seq)
            o = _diff_attn(qk, vt, a_lambda_q1[j], a_lambda_k1[j], a_lambda_q2[j], a_lambda_k2[j],
                           a_subln_g[j], lam_init, batch, seq)
            xf = _mix_out(xf, o, qm, 0, kt, v, w_out[i], ln2_g[i], ln2_b[i], seq)
        else:
            h = _in_proj_b(xf, b_w_in[j])
            o = _win_attn(h, b_sink[j], batch, seq)
            xf = _mix_out(xf, o, h, (TOK_WIDTH + 2 * WIN_KV_WIDTH) // MEM_WIDTH, kt, v, w_out[i],
                          ln2_g[i], ln2_b[i], seq)
        xf = _ffn_ln(xf, ffn2_w13[i], ffn2_w2[i], ln3_g[i], ln3_b[i])
    return xf.reshape(batch, seq, D_MODEL)
```

```python
import functools
import math

import numpy as np
import jax
import jax.numpy as jnp
from jax import lax
from jax.experimental import pallas as pl
from jax.experimental.pallas import tpu as pltpu

D_MODEL = 1024
DEPTH = 2
N_MIXERS = 2
N_MEM = 256
HEAD_DIM = 64
MEM_HEADS = 4
MEM_WIDTH = MEM_HEADS * HEAD_DIM
TOK_WIDTH = D_MODEL - MEM_WIDTH
DIFF_HEADS = TOK_WIDTH // (2 * HEAD_DIM)
DIFF_V_DIM = 2 * HEAD_DIM
QK_A = DIFF_HEADS * 2 * HEAD_DIM
WIN_Q_HEADS = TOK_WIDTH // HEAD_DIM
WIN_GROUP = 3
WIN_KV_HEADS = WIN_Q_HEADS // WIN_GROUP
WIN_KV_WIDTH = WIN_KV_HEADS * HEAD_DIM
WINDOW = 128
D_FF = 2816
A_IN = 2 * QK_A + TOK_WIDTH + MEM_WIDTH
B_IN = TOK_WIDTH + 2 * WIN_KV_WIDTH + MEM_WIDTH
N_A = (DEPTH + 1) // 2
DN_ALPHA = (2 * DEPTH) ** 0.25
LN_EPS = 1e-5
NEG = -1e30
QK_SCALE = HEAD_DIM ** -0.5

F32 = jnp.float32
BF16 = jnp.bfloat16

V7X_LANES = 128
V7X_MXU_DIM = 256
V7X_VMEM_BYTES = 64 * 1024 * 1024

FFN_TM = 512
FF_CHUNK = V7X_MXU_DIM
PROJ_TM = 1024
PROJ_CHUNK = V7X_MXU_DIM
DIFF_TQ = 256
DIFF_TK = 256
DIFF_DEPTH = 8
MIX_TM = 512
WIN_TQ = 512


def _vmem_limit(nbytes):
    return int(min(V7X_VMEM_BYTES - 8 * 1024 * 1024, nbytes * 3 // 2))


def _resident(block_shape, index_map):
    return pl.BlockSpec(block_shape, index_map, pipeline_mode=pl.Buffered(1))


def _layer_norm(y, g, b):
    mu = jnp.mean(y, axis=-1, keepdims=True)
    yc = y - mu
    var = jnp.mean(yc * yc, axis=-1, keepdims=True)
    return yc * lax.rsqrt(var + LN_EPS) * g + b


def _alibi_slopes(n):
    return jnp.asarray(2.0 ** (-8.0 * np.arange(1, n + 1) / n), dtype=F32)


def _ffn_ln_kernel(x_ref, wg_ref, wu_ref, w2_ref, g_ref, b_ref, o_ref, acc_ref):
    x = x_ref[...]
    xb = x.astype(BF16)
    for c in range(D_FF // FF_CHUNK):
        cols = slice(c * FF_CHUNK, (c + 1) * FF_CHUNK)
        gate = jnp.dot(xb, wg_ref[:, cols], preferred_element_type=F32)
        up = jnp.dot(xb, wu_ref[:, cols], preferred_element_type=F32)
        act = (gate / (1.0 + jnp.exp(-gate)) * up).astype(BF16)
        part = jnp.dot(act, w2_ref[cols, :], preferred_element_type=F32)
        if c == 0:
            acc_ref[...] = part
        else:
            acc_ref[...] += part
    y = DN_ALPHA * x + 0.5 * acc_ref[...]
    o_ref[...] = _layer_norm(y, g_ref[...], b_ref[...])


def _ffn_ln(x, w13, w2, g, b):
    n = x.shape[0]
    w13 = w13.astype(BF16)
    w2 = w2.astype(BF16)
    est = (2 * D_MODEL * D_FF + D_FF * D_MODEL) * 2 + 5 * FFN_TM * D_MODEL * 4 + 4 * FFN_TM * FF_CHUNK * 4
    return pl.pallas_call(
        _ffn_ln_kernel,
        out_shape=jax.ShapeDtypeStruct((n, D_MODEL), F32),
        grid=(n // FFN_TM,),
        in_specs=[
            pl.BlockSpec((FFN_TM, D_MODEL), lambda i: (i, 0)),
            _resident((D_MODEL, D_FF), lambda i: (0, 0)),
            _resident((D_MODEL, D_FF), lambda i: (0, 1)),
            _resident((D_FF, D_MODEL), lambda i: (0, 0)),
            _resident((1, D_MODEL), lambda i: (0, 0)),
            _resident((1, D_MODEL), lambda i: (0, 0)),
        ],
        out_specs=pl.BlockSpec((FFN_TM, D_MODEL), lambda i: (i, 0)),
        scratch_shapes=[pltpu.VMEM((FFN_TM, D_MODEL), F32)],
        compiler_params=pltpu.CompilerParams(
            dimension_semantics=("arbitrary",), vmem_limit_bytes=_vmem_limit(est)),
        name="ffn_ln",
    )(x, w13, w13, w2, g.reshape(1, D_MODEL), b.reshape(1, D_MODEL))


def _in_proj_a_kernel(x_ref, w_ref, q_ref, k_ref, vt_ref, qm_ref):
    xb = x_ref[...].astype(BF16)
    heads_per_chunk = PROJ_CHUNK // DIFF_V_DIM
    n_q = QK_A // PROJ_CHUNK
    n_qk = 2 * n_q
    n_v = TOK_WIDTH // PROJ_CHUNK
    for c in range(A_IN // PROJ_CHUNK):
        r = jnp.dot(xb, w_ref[:, c * PROJ_CHUNK:(c + 1) * PROJ_CHUNK], preferred_element_type=F32)
        if c < n_qk:
            dst, c0, scale = (q_ref, 0, QK_SCALE) if c < n_q else (k_ref, n_q, 1.0)
            for hh in range(heads_per_chunk):
                dst[(c - c0) * heads_per_chunk + hh] = (
                    r[:, hh * DIFF_V_DIM:(hh + 1) * DIFF_V_DIM] * scale).astype(BF16)
        elif c < n_qk + n_v:
            rt = r.T.astype(BF16)
            for hh in range(heads_per_chunk):
                for t in range(PROJ_TM // DIFF_TK):
                    vt_ref[(c - n_qk) * heads_per_chunk + hh, t] = rt[
                        hh * DIFF_V_DIM:(hh + 1) * DIFF_V_DIM, t * DIFF_TK:(t + 1) * DIFF_TK]
        else:
            qm_ref[...] = (r * QK_SCALE).astype(BF16)


def _in_proj_a(x, w_in, batch, seq):
    n = x.shape[0]
    w_in = w_in.astype(BF16)
    tiles_per_batch = seq // PROJ_TM
    chunks_per_tile = PROJ_TM // DIFF_TK
    est = D_MODEL * A_IN * 2 + 2 * PROJ_TM * D_MODEL * 4 + 2 * PROJ_TM * A_IN * 2 + 4 * PROJ_TM * PROJ_CHUNK * 4
    head_major = jax.ShapeDtypeStruct((batch, DIFF_HEADS, seq, DIFF_V_DIM), BF16)
    head_major_spec = pl.BlockSpec((None, DIFF_HEADS, PROJ_TM, DIFF_V_DIM),
                                   lambda i: (i // tiles_per_batch, 0, i % tiles_per_batch, 0))
    return pl.pallas_call(
        _in_proj_a_kernel,
        out_shape=(
            head_major,
            head_major,
            jax.ShapeDtypeStruct((batch, DIFF_HEADS, seq // DIFF_TK, DIFF_V_DIM, DIFF_TK), BF16),
            jax.ShapeDtypeStruct((n, MEM_WIDTH), BF16),
        ),
        grid=(n // PROJ_TM,),
        in_specs=[
            pl.BlockSpec((PROJ_TM, D_MODEL), lambda i: (i, 0)),
            _resident((D_MODEL, A_IN), lambda i: (0, 0)),
        ],
        out_specs=(
            head_major_spec,
            head_major_spec,
            pl.BlockSpec((None, DIFF_HEADS, chunks_per_tile, DIFF_V_DIM, DIFF_TK),
                         lambda i: (i // tiles_per_batch, 0, i % tiles_per_batch, 0, 0)),
            pl.BlockSpec((PROJ_TM, MEM_WIDTH), lambda i: (i, 0)),
        ),
        compiler_params=pltpu.CompilerParams(
            dimension_semantics=("arbitrary",), vmem_limit_bytes=_vmem_limit(est)),
        name="in_proj_a",
    )(x, w_in)


def _diff_attn_kernel(slopes_ref, q_ref, k_ref, vt_ref, lq1_ref, lk1_ref, lq2_ref, lk2_ref, g_ref,
                      o_ref, qt_ref, pos_ref, bias_ref, sr_ref, m_ref, l_ref, acc_ref,
                      st_ref, shift_ref, max_ref, p_ref, alpha_ref, *, lam_init, seq):
    h = pl.program_id(1)
    i = pl.program_id(2)
    tq, tk = DIFF_TQ, DIFF_TK
    assert tq == tk and tk <= 256
    n_steps = seq // tk
    n_strips = 2 * tq // V7X_LANES
    slope = slopes_ref[h]

    qt = q_ref[...].astype(F32).T
    row = lax.broadcasted_iota(jnp.int32, qt.shape, 0)
    zero = jnp.zeros_like(qt)
    qt_ref[0:DIFF_V_DIM, :tq] = jnp.where(row < HEAD_DIM, qt, zero).astype(BF16)
    qt_ref[0:DIFF_V_DIM, tq:] = jnp.where(row >= HEAD_DIM, qt, zero).astype(BF16)
    slope_v = jnp.full((16, 2 * tq), slope, F32)
    s_hi = slope_v.astype(BF16).astype(F32)
    s_mid = (slope_v - s_hi).astype(BF16).astype(F32)
    s_lo = slope_v - s_hi - s_mid
    prow = lax.broadcasted_iota(jnp.int32, (16, 2 * tq), 0)
    pieces = jnp.where(prow == 0, s_hi, jnp.where(prow == 1, s_mid, jnp.where(prow == 2, s_lo, 0.0)))
    qt_ref[DIFF_V_DIM:DIFF_V_DIM + 16, :] = pieces.astype(BF16)
    qt_ref[DIFF_V_DIM + 16:, :] = jnp.zeros((DIFF_V_DIM - 16, 2 * tq), BF16)

    r_s = lax.broadcasted_iota(jnp.int32, (tk, V7X_LANES), 0).astype(F32)
    lane = lax.broadcasted_iota(jnp.int32, (tk, V7X_LANES), 1)
    r_s = jnp.where(lane < 3, r_s, 0.0)
    pos_ref[0] = r_s.astype(BF16)
    pos_ref[1] = (-r_s).astype(BF16)
    pos_ref[2] = jnp.zeros((tk, V7X_LANES), BF16)

    rel = (lax.broadcasted_iota(jnp.int32, (tk, tq), 0) - lax.broadcasted_iota(jnp.int32, (tk, tq), 1))
    bias_ref[...] = slope * jnp.abs(rel).astype(F32)
    r_t = lax.broadcasted_iota(jnp.int32, (1, 2 * tq), 1)
    sr_ref[...] = slope * jnp.where(r_t >= tq, r_t - tq, r_t).astype(F32)

    m_ref[...] = jnp.full(m_ref.shape, -jnp.inf, F32)
    l_ref[...] = jnp.zeros(l_ref.shape, F32)
    acc_ref[...] = jnp.zeros(acc_ref.shape, F32)

    def keys(j):
        return k_ref[pl.ds(pl.multiple_of(j * tk, tk), tk), :]

    def tile_of(t):
        return jnp.where(t == 0, i, jnp.where(t > i, t, t - 1))

    def strip(c):
        return slice(c * V7X_LANES, (c + 1) * V7X_LANES)

    def scores(t, b):
        j = tile_of(t)
        diag = isinstance(t, int) and t == 0
        if diag:
            k_aug = jnp.concatenate([keys(j), pos_ref[2]], axis=1)
            shift_ref[b] = jnp.zeros((1, 2 * tq), F32)
        else:
            after = (j > i).astype(jnp.int32)
            k_aug = jnp.concatenate([keys(j), pos_ref[after]], axis=1)
            sign = (1 - 2 * after).astype(F32)
            tile_dist = (jnp.abs(j - i) * tk).astype(F32)
            shift_ref[b] = -(sign * sr_ref[...]) - slope * tile_dist
        st = jnp.dot(k_aug, qt_ref[...], preferred_element_type=F32)
        for c in range(n_strips):
            blk = st[:, strip(c)]
            if diag:
                blk = blk - bias_ref[:, strip(c % (tq // V7X_LANES))]
            st_ref[b, c] = blk
            max_ref[b, :, strip(c)] = jnp.max(blk, axis=0, keepdims=True)

    def softmax(b):
        for c in range(n_strips):
            lanes = strip(c)
            shift = shift_ref[b, :, lanes]
            m_old = m_ref[:, lanes]
            m_new = jnp.maximum(m_old, max_ref[b, :, lanes] + shift)
            alpha = jnp.exp(m_old - m_new)
            p = jnp.exp(st_ref[b, c] - (m_new - shift))
            l_ref[:, lanes] = alpha * l_ref[:, lanes] + jnp.sum(p, axis=0, keepdims=True)
            m_ref[:, lanes] = m_new
            p_ref[b, c] = p.astype(BF16)
            alpha_ref[b, :, lanes] = alpha

    def pv(t, b):
        p = jnp.concatenate([p_ref[b, c] for c in range(n_strips)], axis=1)
        acc_ref[...] = alpha_ref[b] * acc_ref[...] + jnp.dot(vt_ref[tile_of(t)], p,
                                                              preferred_element_type=F32)

    depth = DIFF_DEPTH
    assert (n_steps - 2 * depth) % depth == 0

    def stage(t, r):
        if not (isinstance(t, int) and t < depth):
            pv(t - depth, r)
        softmax(r)
        if not (isinstance(t, int) and t + depth >= n_steps):
            scores(t + depth, r)

    for t in range(depth):
        scores(t, t)
    for t in range(depth):
        stage(t, t)

    def body(k, carry):
        for r in range(depth):
            stage(depth * (k + 1) + r, r)
        return carry

    lax.fori_loop(0, (n_steps - 2 * depth) // depth, body, 0)
    for t in range(n_steps - depth, n_steps):
        stage(t, t % depth)
    for t in range(n_steps - depth, n_steps):
        pv(t, t % depth)

    lam = (jnp.exp(jnp.sum(lq1_ref[...] * lk1_ref[...], axis=-1, keepdims=True))
           - jnp.exp(jnp.sum(lq2_ref[...] * lk2_ref[...], axis=-1, keepdims=True)) + lam_init)
    o_t = acc_ref[...] / l_ref[...]
    a = (o_t[:, :tq] - lam * o_t[:, tq:]).T
    a = a * lax.rsqrt(jnp.mean(a * a, axis=-1, keepdims=True) + LN_EPS) * g_ref[...]
    o_ref[...] = (a * (1.0 - lam_init)).astype(BF16)


def _diff_attn(q, k, vt, lq1, lk1, lq2, lk2, subln_g, lam_init, batch, seq):
    nq = seq // DIFF_TQ
    strips = (2 * DIFF_TQ // V7X_LANES, DIFF_TK, V7X_LANES)
    slopes = _alibi_slopes(DIFF_HEADS)
    vec = lambda v: v.reshape(1, -1).astype(F32)
    est = (2 * 2 * seq * DIFF_V_DIM * 2 + 4 * DIFF_TQ * DIFF_V_DIM * 2
           + DIFF_DEPTH * DIFF_TK * 2 * DIFF_TQ * (4 + 2)
           + 4 * DIFF_TK * 2 * DIFF_TQ * 4 + 2 * DIFF_V_DIM * 2 * DIFF_TQ * 4)
    small = lambda n: pl.BlockSpec((1, n), lambda b, h, i: (0, 0))
    return pl.pallas_call(
        functools.partial(_diff_attn_kernel, lam_init=lam_init, seq=seq),
        out_shape=jax.ShapeDtypeStruct((batch * seq, TOK_WIDTH), BF16),
        grid=(batch, DIFF_HEADS, nq),
        in_specs=[
            pl.BlockSpec(memory_space=pltpu.SMEM),
            pl.BlockSpec((None, None, DIFF_TQ, DIFF_V_DIM), lambda b, h, i: (b, h, i, 0)),
            pl.BlockSpec((None, None, seq, DIFF_V_DIM), lambda b, h, i: (b, h, 0, 0)),
            pl.BlockSpec((None, None, seq // DIFF_TK, DIFF_V_DIM, DIFF_TK), lambda b, h, i: (b, h, 0, 0, 0)),
            small(HEAD_DIM), small(HEAD_DIM), small(HEAD_DIM), small(HEAD_DIM), small(DIFF_V_DIM),
        ],
        out_specs=pl.BlockSpec((DIFF_TQ, DIFF_V_DIM), lambda b, h, i: (b * nq + i, h)),
        scratch_shapes=[
            pltpu.VMEM((2 * DIFF_V_DIM, 2 * DIFF_TQ), BF16),
            pltpu.VMEM((3, DIFF_TK, V7X_LANES), BF16),
            pltpu.VMEM((DIFF_TK, DIFF_TQ), F32),
            pltpu.VMEM((1, 2 * DIFF_TQ), F32),
            pltpu.VMEM((1, 2 * DIFF_TQ), F32),
            pltpu.VMEM((1, 2 * DIFF_TQ), F32),
            pltpu.VMEM((DIFF_V_DIM, 2 * DIFF_TQ), F32),
            pltpu.VMEM((DIFF_DEPTH,) + strips, F32),
            pltpu.VMEM((DIFF_DEPTH, 1, 2 * DIFF_TQ), F32),
            pltpu.VMEM((DIFF_DEPTH, 1, 2 * DIFF_TQ), F32),
            pltpu.VMEM((DIFF_DEPTH,) + strips, BF16),
            pltpu.VMEM((DIFF_DEPTH, 1, 2 * DIFF_TQ), F32),
        ],
        compiler_params=pltpu.CompilerParams(
            dimension_semantics=("arbitrary", "arbitrary", "arbitrary"), vmem_limit_bytes=_vmem_limit(est)),
        name="diff_attn",
    )(slopes, q, k, vt, vec(lq1), vec(lk1), vec(lq2), vec(lk2), vec(subln_g))


def _mem_kv_kernel(mem_ref, w_ref, kt_ref, v_ref):
    kv = jnp.dot(mem_ref[...].astype(BF16), w_ref[...], preferred_element_type=F32)
    kt = kv[:, :MEM_WIDTH].T
    v = kv[:, MEM_WIDTH:]
    pair_rows = 2 * HEAD_DIM
    row = lax.broadcasted_iota(jnp.int32, (pair_rows, N_MEM), 0)
    col = lax.broadcasted_iota(jnp.int32, (N_MEM, MEM_WIDTH), 1)
    for h in range(MEM_HEADS):
        pair = kt[(h // 2) * pair_rows:(h // 2 + 1) * pair_rows, :]
        keep = (row >= (h % 2) * HEAD_DIM) & (row < (h % 2 + 1) * HEAD_DIM)
        kt_ref[h] = jnp.where(keep, pair, jnp.zeros_like(pair)).astype(BF16)
        keep_v = (col >= h * HEAD_DIM) & (col < (h + 1) * HEAD_DIM)
        v_ref[h] = jnp.where(keep_v, v, jnp.zeros_like(v)).astype(BF16)


def _mem_kv(mem, w_mem_kv):
    batch = mem.shape[0]
    return pl.pallas_call(
        _mem_kv_kernel,
        out_shape=(
            jax.ShapeDtypeStruct((batch, MEM_HEADS, 2 * HEAD_DIM, N_MEM), BF16),
            jax.ShapeDtypeStruct((batch, MEM_HEADS, N_MEM, MEM_WIDTH), BF16),
        ),
        grid=(batch,),
        in_specs=[
            pl.BlockSpec((None, N_MEM, D_MODEL), lambda b: (b, 0, 0)),
            pl.BlockSpec((D_MODEL, 2 * MEM_WIDTH), lambda b: (0, 0)),
        ],
        out_specs=(
            pl.BlockSpec((None, MEM_HEADS, 2 * HEAD_DIM, N_MEM), lambda b: (b, 0, 0, 0)),
            pl.BlockSpec((None, MEM_HEADS, N_MEM, MEM_WIDTH), lambda b: (b, 0, 0, 0)),
        ),
        name="mem_kv",
    )(mem, w_mem_kv.astype(BF16))


def _mix_out_kernel(x_ref, o_ref, qm_ref, kt_ref, v_ref, wo_ref, wm_ref, g_ref, b_ref, out_ref):
    qm = qm_ref[...]
    pair_w = 2 * HEAD_DIM
    mem_o = None
    for h in range(MEM_HEADS):
        q_pair = qm[:, (h // 2) * pair_w:(h // 2 + 1) * pair_w]
        s = jnp.dot(q_pair, kt_ref[h], preferred_element_type=F32)
        e = jnp.exp(s - jnp.max(s, axis=-1, keepdims=True))
        p = (e / jnp.sum(e, axis=-1, keepdims=True)).astype(BF16)
        part = jnp.dot(p, v_ref[h], preferred_element_type=F32)
        mem_o = part if mem_o is None else mem_o + part
    y = jnp.dot(o_ref[...], wo_ref[...], preferred_element_type=F32)
    y = y + jnp.dot(mem_o.astype(BF16), wm_ref[...], preferred_element_type=F32)
    out_ref[...] = _layer_norm(DN_ALPHA * x_ref[...] + y, g_ref[...], b_ref[...])


def _mix_out(x, o, qm_arr, qm_col, kt, v, w_out, g, b, seq):
    n = x.shape[0]
    tiles_per_batch = seq // MIX_TM
    w_out = w_out.astype(BF16)
    est = (D_MODEL * D_MODEL * 2 + 4 * MIX_TM * D_MODEL * 4 + 2 * MIX_TM * D_MODEL * 2
           + 4 * MEM_HEADS * N_MEM * MEM_WIDTH * 2 + 6 * MIX_TM * D_MODEL * 4)
    return pl.pallas_call(
        _mix_out_kernel,
        out_shape=jax.ShapeDtypeStruct((n, D_MODEL), F32),
        grid=(n // MIX_TM,),
        in_specs=[
            pl.BlockSpec((MIX_TM, D_MODEL), lambda i: (i, 0)),
            pl.BlockSpec((MIX_TM, TOK_WIDTH), lambda i: (i, 0)),
            pl.BlockSpec((MIX_TM, MEM_WIDTH), lambda i: (i, qm_col)),
            pl.BlockSpec((None, MEM_HEADS, 2 * HEAD_DIM, N_MEM), lambda i: (i // tiles_per_batch, 0, 0, 0)),
            pl.BlockSpec((None, MEM_HEADS, N_MEM, MEM_WIDTH), lambda i: (i // tiles_per_batch, 0, 0, 0)),
            _resident((TOK_WIDTH, D_MODEL), lambda i: (0, 0)),
            _resident((MEM_WIDTH, D_MODEL), lambda i: (TOK_WIDTH // MEM_WIDTH, 0)),
            _resident((1, D_MODEL), lambda i: (0, 0)),
            _resident((1, D_MODEL), lambda i: (0, 0)),
        ],
        out_specs=pl.BlockSpec((MIX_TM, D_MODEL), lambda i: (i, 0)),
        compiler_params=pltpu.CompilerParams(
            dimension_semantics=("arbitrary",), vmem_limit_bytes=_vmem_limit(est)),
        name="mix_out",
    )(x, o, qm_arr, kt, v, w_out, w_out, g.reshape(1, D_MODEL), b.reshape(1, D_MODEL))


def _in_proj_b_kernel(x_ref, w_ref, h_ref):
    xb = x_ref[...].astype(BF16)
    k_lo = TOK_WIDTH // PROJ_CHUNK
    k_hi = (TOK_WIDTH + 2 * WIN_KV_WIDTH) // PROJ_CHUNK
    for c in range(B_IN // PROJ_CHUNK):
        cols = slice(c * PROJ_CHUNK, (c + 1) * PROJ_CHUNK)
        r = jnp.dot(xb, w_ref[:, cols], preferred_element_type=F32)
        if c < k_lo or c >= k_hi:
            r = r * QK_SCALE
        h_ref[:, cols] = r.astype(BF16)


def _in_proj_b(x, w_in):
    n = x.shape[0]
    est = D_MODEL * B_IN * 2 + 2 * PROJ_TM * D_MODEL * 4 + 2 * PROJ_TM * B_IN * 2 + 4 * PROJ_TM * PROJ_CHUNK * 4
    return pl.pallas_call(
        _in_proj_b_kernel,
        out_shape=jax.ShapeDtypeStruct((n, B_IN), BF16),
        grid=(n // PROJ_TM,),
        in_specs=[
            pl.BlockSpec((PROJ_TM, D_MODEL), lambda i: (i, 0)),
            _resident((D_MODEL, B_IN), lambda i: (0, 0)),
        ],
        out_specs=pl.BlockSpec((PROJ_TM, B_IN), lambda i: (i, 0)),
        compiler_params=pltpu.CompilerParams(
            dimension_semantics=("arbitrary",), vmem_limit_bytes=_vmem_limit(est)),
        name="in_proj_b",
    )(x, w_in.astype(BF16))


def _win_attn_kernel(slopes_ref, sink_ref, q_ref, k_ref, v_ref, o_ref, *, seq):
    i = pl.program_id(1)
    span = 3 * WINDOW
    blocks = WIN_TQ // WINDOW
    rel = (lax.broadcasted_iota(jnp.int32, (WINDOW, span), 0)
           - lax.broadcasted_iota(jnp.int32, (WINDOW, span), 1))

    def block(nn, carry):
        t0 = (i * blocks + nn) * WINDOW
        start = pl.multiple_of(jnp.clip(t0 - WINDOW, 0, seq - span), WINDOW)
        dist = jnp.abs(rel + (t0 - start))
        valid = dist <= WINDOW
        dist = dist.astype(F32)
        kwin = k_ref[pl.ds(start, span), :]
        vwin = v_ref[pl.ds(start, span), :]
        rows = pl.ds(pl.multiple_of(nn * WINDOW, WINDOW), WINDOW)
        outs = []
        for h in range(WIN_Q_HEADS):
            kv = h // WIN_GROUP
            qh = q_ref[rows, h * HEAD_DIM:(h + 1) * HEAD_DIM]
            kh = kwin[:, kv * HEAD_DIM:(kv + 1) * HEAD_DIM]
            s = lax.dot_general(qh, kh, (((1,), (1,)), ((), ())), preferred_element_type=F32)
            s = jnp.where(valid, s - slopes_ref[h] * dist, NEG)
            sink = sink_ref[h]
            m = jnp.maximum(jnp.max(s, axis=-1, keepdims=True), sink)
            e = jnp.exp(s - m)
            denom = jnp.sum(e, axis=-1, keepdims=True) + jnp.exp(sink - m)
            p = (e / denom).astype(BF16)
            outs.append(jnp.dot(p, vwin[:, kv * HEAD_DIM:(kv + 1) * HEAD_DIM], preferred_element_type=F32))
        o_ref[rows, :] = jnp.concatenate(outs, axis=-1).astype(BF16)
        return carry

    lax.fori_loop(0, blocks, block, 0)


def _win_attn(h, sink, batch, seq):
    nq = seq // WIN_TQ
    slopes = _alibi_slopes(WIN_Q_HEADS)
    est = 2 * 2 * seq * WIN_KV_WIDTH * 2 + 4 * WIN_TQ * TOK_WIDTH * 2 + 16 * WINDOW * 3 * WINDOW * 4
    return pl.pallas_call(
        functools.partial(_win_attn_kernel, seq=seq),
        out_shape=jax.ShapeDtypeStruct((batch * seq, TOK_WIDTH), BF16),
        grid=(batch, nq),
        in_specs=[
            pl.BlockSpec(memory_space=pltpu.SMEM),
            pl.BlockSpec(memory_space=pltpu.SMEM),
            pl.BlockSpec((WIN_TQ, TOK_WIDTH), lambda b, i: (b * nq + i, 0)),
            pl.BlockSpec((seq, WIN_KV_WIDTH), lambda b, i: (b, TOK_WIDTH // WIN_KV_WIDTH)),
            pl.BlockSpec((seq, WIN_KV_WIDTH), lambda b, i: (b, TOK_WIDTH // WIN_KV_WIDTH + 1)),
        ],
        out_specs=pl.BlockSpec((WIN_TQ, TOK_WIDTH), lambda b, i: (b * nq + i, 0)),
        compiler_params=pltpu.CompilerParams(
            dimension_semantics=("arbitrary", "arbitrary"), vmem_limit_bytes=_vmem_limit(est)),
        name="win_attn",
    )(slopes, sink.astype(F32), h, h, h)


def kernel(x, mem, ffn1_w13, ffn1_w2, ln1_g, ln1_b, w_mem_kv, w_out, ln2_g, ln2_b, ffn2_w13, ffn2_w2,
           ln3_g, ln3_b, a_w_in, a_lambda_q1, a_lambda_k1, a_lambda_q2, a_lambda_k2, a_subln_g, b_w_in, b_sink):
    batch, seq, _ = x.shape
    xf = x.reshape(batch * seq, D_MODEL)
    for i in range(DEPTH):
        xf = _ffn_ln(xf, ffn1_w13[i], ffn1_w2[i], ln1_g[i], ln1_b[i])
        j = i // N_MIXERS
        kt, v = _mem_kv(mem, w_mem_kv[i])
        if i % N_MIXERS == 0:
            lam_init = 0.8 - 0.6 * math.exp(-0.3 * i)
            q, k, vt, qm = _in_proj_a(xf, a_w_in[j], batch, seq)
            o = _diff_attn(q, k, vt, a_lambda_q1[j], a_lambda_k1[j], a_lambda_q2[j], a_lambda_k2[j],
                           a_subln_g[j], lam_init, batch, seq)
            xf = _mix_out(xf, o, qm, 0, kt, v, w_out[i], ln2_g[i], ln2_b[i], seq)
        else:
            h = _in_proj_b(xf, b_w_in[j])
            o = _win_attn(h, b_sink[j], batch, seq)
            xf = _mix_out(xf, o, h, (TOK_WIDTH + 2 * WIN_KV_WIDTH) // MEM_WIDTH, kt, v, w_out[i],
                          ln2_g[i], ln2_b[i], seq)
        xf = _ffn_ln(xf, ffn2_w13[i], ffn2_w2[i], ln3_g[i], ln3_b[i])
    return xf.reshape(batch, seq, D_MODEL)
```

```python
import functools
import math

import numpy as np
import jax
import jax.numpy as jnp
from jax import lax
from jax.experimental import pallas as pl
from jax.experimental.pallas import tpu as pltpu

D_MODEL = 1024
DEPTH = 2
N_MIXERS = 2
N_MEM = 256
HEAD_DIM = 64
MEM_HEADS = 4
MEM_WIDTH = MEM_HEADS * HEAD_DIM
TOK_WIDTH = D_MODEL - MEM_WIDTH
DIFF_HEADS = TOK_WIDTH // (2 * HEAD_DIM)
DIFF_V_DIM = 2 * HEAD_DIM
QK_A = DIFF_HEADS * 2 * HEAD_DIM
WIN_Q_HEADS = TOK_WIDTH // HEAD_DIM
WIN_GROUP = 3
WIN_KV_HEADS = WIN_Q_HEADS // WIN_GROUP
WIN_KV_WIDTH = WIN_KV_HEADS * HEAD_DIM
WINDOW = 128
D_FF = 2816
A_IN = 2 * QK_A + TOK_WIDTH + MEM_WIDTH
B_IN = TOK_WIDTH + 2 * WIN_KV_WIDTH + MEM_WIDTH
N_A = (DEPTH + 1) // 2
DN_ALPHA = (2 * DEPTH) ** 0.25
LN_EPS = 1e-5
NEG = -1e30
QK_SCALE = HEAD_DIM ** -0.5
LOG2E = math.log2(math.e)

F32 = jnp.float32
BF16 = jnp.bfloat16

V7X_LANES = 128
V7X_MXU_DIM = 256
V7X_VMEM_BYTES = 64 * 1024 * 1024

FFN_TM = 512
FF_CHUNK = V7X_MXU_DIM
PROJ_TM = 1024
PROJ_CHUNK = V7X_MXU_DIM
DIFF_TQ = 256
DIFF_TK = 256
DIFF_DEPTH = 8
MIX_TM = 512
WIN_TQ = 1024


def _vmem_limit(nbytes):
    return int(min(V7X_VMEM_BYTES - 8 * 1024 * 1024, nbytes * 3 // 2))


def _resident(block_shape, index_map):
    return pl.BlockSpec(block_shape, index_map, pipeline_mode=pl.Buffered(1))


def _layer_norm(y, g, b):
    mu = jnp.mean(y, axis=-1, keepdims=True)
    yc = y - mu
    var = jnp.mean(yc * yc, axis=-1, keepdims=True)
    return yc * lax.rsqrt(var + LN_EPS) * g + b


def _alibi_slopes(n):
    return jnp.asarray(2.0 ** (-8.0 * np.arange(1, n + 1) / n), dtype=F32)


def _ffn_ln_kernel(x_ref, wg_ref, wu_ref, w2_ref, g_ref, b_ref, o_ref, acc_ref):
    x = x_ref[...]
    xb = x.astype(BF16)
    for c in range(D_FF // FF_CHUNK):
        cols = slice(c * FF_CHUNK, (c + 1) * FF_CHUNK)
        gate = jnp.dot(xb, wg_ref[:, cols], preferred_element_type=F32)
        up = jnp.dot(xb, wu_ref[:, cols], preferred_element_type=F32)
        act = (gate / (1.0 + jnp.exp(-gate)) * up).astype(BF16)
        part = jnp.dot(act, w2_ref[cols, :], preferred_element_type=F32)
        if c == 0:
            acc_ref[...] = part
        else:
            acc_ref[...] += part
    y = DN_ALPHA * x + 0.5 * acc_ref[...]
    o_ref[...] = _layer_norm(y, g_ref[...], b_ref[...])


def _ffn_ln(x, w13, w2, g, b):
    n = x.shape[0]
    w13 = w13.astype(BF16)
    w2 = w2.astype(BF16)
    est = (2 * D_MODEL * D_FF + D_FF * D_MODEL) * 2 + 5 * FFN_TM * D_MODEL * 4 + 4 * FFN_TM * FF_CHUNK * 4
    return pl.pallas_call(
        _ffn_ln_kernel,
        out_shape=jax.ShapeDtypeStruct((n, D_MODEL), F32),
        grid=(n // FFN_TM,),
        in_specs=[
            pl.BlockSpec((FFN_TM, D_MODEL), lambda i: (i, 0)),
            _resident((D_MODEL, D_FF), lambda i: (0, 0)),
            _resident((D_MODEL, D_FF), lambda i: (0, 1)),
            _resident((D_FF, D_MODEL), lambda i: (0, 0)),
            _resident((1, D_MODEL), lambda i: (0, 0)),
            _resident((1, D_MODEL), lambda i: (0, 0)),
        ],
        out_specs=pl.BlockSpec((FFN_TM, D_MODEL), lambda i: (i, 0)),
        scratch_shapes=[pltpu.VMEM((FFN_TM, D_MODEL), F32)],
        compiler_params=pltpu.CompilerParams(
            dimension_semantics=("arbitrary",), vmem_limit_bytes=_vmem_limit(est)),
        name="ffn_ln",
    )(x, w13, w13, w2, g.reshape(1, D_MODEL), b.reshape(1, D_MODEL))


def _in_proj_a_kernel(x_ref, w_ref, q_ref, k_ref, vt_ref, qm_ref):
    xb = x_ref[...].astype(BF16)
    heads_per_chunk = PROJ_CHUNK // DIFF_V_DIM
    n_q = QK_A // PROJ_CHUNK
    n_qk = 2 * n_q
    n_v = TOK_WIDTH // PROJ_CHUNK
    for c in range(A_IN // PROJ_CHUNK):
        r = jnp.dot(xb, w_ref[:, c * PROJ_CHUNK:(c + 1) * PROJ_CHUNK], preferred_element_type=F32)
        if c < n_qk:
            dst, c0, scale = (q_ref, 0, QK_SCALE * LOG2E) if c < n_q else (k_ref, n_q, 1.0)
            for hh in range(heads_per_chunk):
                dst[(c - c0) * heads_per_chunk + hh] = (
                    r[:, hh * DIFF_V_DIM:(hh + 1) * DIFF_V_DIM] * scale).astype(BF16)
        elif c < n_qk + n_v:
            rt = r.T.astype(BF16)
            for hh in range(heads_per_chunk):
                for t in range(PROJ_TM // DIFF_TK):
                    vt_ref[(c - n_qk) * heads_per_chunk + hh, t] = rt[
                        hh * DIFF_V_DIM:(hh + 1) * DIFF_V_DIM, t * DIFF_TK:(t + 1) * DIFF_TK]
        else:
            qm_ref[...] = (r * QK_SCALE).astype(BF16)


def _in_proj_a(x, w_in, batch, seq):
    n = x.shape[0]
    w_in = w_in.astype(BF16)
    tiles_per_batch = seq // PROJ_TM
    chunks_per_tile = PROJ_TM // DIFF_TK
    est = D_MODEL * A_IN * 2 + 2 * PROJ_TM * D_MODEL * 4 + 2 * PROJ_TM * A_IN * 2 + 4 * PROJ_TM * PROJ_CHUNK * 4
    head_major = jax.ShapeDtypeStruct((batch, DIFF_HEADS, seq, DIFF_V_DIM), BF16)
    head_major_spec = pl.BlockSpec((None, DIFF_HEADS, PROJ_TM, DIFF_V_DIM),
                                   lambda i: (i // tiles_per_batch, 0, i % tiles_per_batch, 0))
    return pl.pallas_call(
        _in_proj_a_kernel,
        out_shape=(
            head_major,
            head_major,
            jax.ShapeDtypeStruct((batch, DIFF_HEADS, seq // DIFF_TK, DIFF_V_DIM, DIFF_TK), BF16),
            jax.ShapeDtypeStruct((n, MEM_WIDTH), BF16),
        ),
        grid=(n // PROJ_TM,),
        in_specs=[
            pl.BlockSpec((PROJ_TM, D_MODEL), lambda i: (i, 0)),
            _resident((D_MODEL, A_IN), lambda i: (0, 0)),
        ],
        out_specs=(
            head_major_spec,
            head_major_spec,
            pl.BlockSpec((None, DIFF_HEADS, chunks_per_tile, DIFF_V_DIM, DIFF_TK),
                         lambda i: (i // tiles_per_batch, 0, i % tiles_per_batch, 0, 0)),
            pl.BlockSpec((PROJ_TM, MEM_WIDTH), lambda i: (i, 0)),
        ),
        compiler_params=pltpu.CompilerParams(
            dimension_semantics=("arbitrary",), vmem_limit_bytes=_vmem_limit(est)),
        name="in_proj_a",
    )(x, w_in)


def _diff_attn_kernel(slopes_ref, q_ref, k_ref, vt_ref, lq1_ref, lk1_ref, lq2_ref, lk2_ref, g_ref,
                      o_ref, qt_ref, pos_ref, bias_ref, sr_ref, m_ref, l_ref, acc_ref,
                      st_ref, shift_ref, max_ref, p_ref, alpha_ref, *, lam_init, seq):
    h = pl.program_id(1)
    i = pl.program_id(2)
    tq, tk = DIFF_TQ, DIFF_TK
    assert tq == tk and tk <= 256
    n_steps = seq // tk
    n_strips = 2 * tq // V7X_LANES
    slope = slopes_ref[h]

    qt = q_ref[...].astype(F32).T
    row = lax.broadcasted_iota(jnp.int32, qt.shape, 0)
    zero = jnp.zeros_like(qt)
    qt_ref[0:DIFF_V_DIM, :tq] = jnp.where(row < HEAD_DIM, qt, zero).astype(BF16)
    qt_ref[0:DIFF_V_DIM, tq:] = jnp.where(row >= HEAD_DIM, qt, zero).astype(BF16)
    slope_v = jnp.full((16, 2 * tq), slope, F32)
    s_hi = slope_v.astype(BF16).astype(F32)
    s_mid = (slope_v - s_hi).astype(BF16).astype(F32)
    s_lo = slope_v - s_hi - s_mid
    prow = lax.broadcasted_iota(jnp.int32, (16, 2 * tq), 0)
    pieces = jnp.where(prow == 0, s_hi, jnp.where(prow == 1, s_mid, jnp.where(prow == 2, s_lo, 0.0)))
    qt_ref[DIFF_V_DIM:DIFF_V_DIM + 16, :] = pieces.astype(BF16)
    qt_ref[DIFF_V_DIM + 16:, :] = jnp.zeros((DIFF_V_DIM - 16, 2 * tq), BF16)

    r_s = lax.broadcasted_iota(jnp.int32, (tk, V7X_LANES), 0).astype(F32)
    lane = lax.broadcasted_iota(jnp.int32, (tk, V7X_LANES), 1)
    r_s = jnp.where(lane < 3, r_s, 0.0)
    pos_ref[0] = r_s.astype(BF16)
    pos_ref[1] = (-r_s).astype(BF16)
    pos_ref[2] = jnp.zeros((tk, V7X_LANES), BF16)

    rel = (lax.broadcasted_iota(jnp.int32, (tk, tq), 0) - lax.broadcasted_iota(jnp.int32, (tk, tq), 1))
    bias_ref[...] = slope * jnp.abs(rel).astype(F32)
    r_t = lax.broadcasted_iota(jnp.int32, (1, 2 * tq), 1)
    sr_ref[...] = slope * jnp.where(r_t >= tq, r_t - tq, r_t).astype(F32)

    m_ref[...] = jnp.full(m_ref.shape, -jnp.inf, F32)
    l_ref[...] = jnp.zeros(l_ref.shape, F32)
    acc_ref[...] = jnp.zeros(acc_ref.shape, F32)

    def keys(j):
        return k_ref[pl.ds(pl.multiple_of(j * tk, tk), tk), :]

    def tile_of(t):
        return jnp.where(t == 0, i, jnp.where(t > i, t, t - 1))

    def strip(c):
        return slice(c * V7X_LANES, (c + 1) * V7X_LANES)

    def scores(t, b):
        j = tile_of(t)
        diag = isinstance(t, int) and t == 0
        if diag:
            k_aug = jnp.concatenate([keys(j), pos_ref[2]], axis=1)
            shift_ref[b] = jnp.zeros((1, 2 * tq), F32)
        else:
            after = (j > i).astype(jnp.int32)
            k_aug = jnp.concatenate([keys(j), pos_ref[after]], axis=1)
            sign = (1 - 2 * after).astype(F32)
            tile_dist = (jnp.abs(j - i) * tk).astype(F32)
            shift_ref[b] = -(sign * sr_ref[...]) - slope * tile_dist
        st = jnp.dot(k_aug, qt_ref[...], preferred_element_type=F32)
        for c in range(n_strips):
            blk = st[:, strip(c)]
            if diag:
                blk = blk - bias_ref[:, strip(c % (tq // V7X_LANES))]
            st_ref[b, c] = blk
            max_ref[b, :, strip(c)] = jnp.max(blk, axis=0, keepdims=True)

    def softmax(b):
        for c in range(n_strips):
            lanes = strip(c)
            shift = shift_ref[b, :, lanes]
            m_old = m_ref[:, lanes]
            m_new = jnp.maximum(m_old, max_ref[b, :, lanes] + shift)
            alpha = jnp.exp2(m_old - m_new)
            p = jnp.exp2(st_ref[b, c] - (m_new - shift))
            l_ref[:, lanes] = alpha * l_ref[:, lanes] + jnp.sum(p, axis=0, keepdims=True)
            m_ref[:, lanes] = m_new
            p_ref[b, c] = p.astype(BF16)
            alpha_ref[b, :, lanes] = alpha

    def pv(t, b):
        p = jnp.concatenate([p_ref[b, c] for c in range(n_strips)], axis=1)
        acc_ref[...] = alpha_ref[b] * acc_ref[...] + jnp.dot(vt_ref[tile_of(t)], p,
                                                              preferred_element_type=F32)

    depth = DIFF_DEPTH
    assert (n_steps - 2 * depth) % depth == 0

    def stage(t, r):
        if not (isinstance(t, int) and t < depth):
            pv(t - depth, r)
        softmax(r)
        if not (isinstance(t, int) and t + depth >= n_steps):
            scores(t + depth, r)

    for t in range(depth):
        scores(t, t)
    for t in range(depth):
        stage(t, t)

    def body(k, carry):
        for r in range(depth):
            stage(depth * (k + 1) + r, r)
        return carry

    lax.fori_loop(0, (n_steps - 2 * depth) // depth, body, 0)
    for t in range(n_steps - depth, n_steps):
        stage(t, t % depth)
    for t in range(n_steps - depth, n_steps):
        pv(t, t % depth)

    lam = (jnp.exp(jnp.sum(lq1_ref[...] * lk1_ref[...], axis=-1, keepdims=True))
           - jnp.exp(jnp.sum(lq2_ref[...] * lk2_ref[...], axis=-1, keepdims=True)) + lam_init)
    o_t = acc_ref[...] / l_ref[...]
    a = (o_t[:, :tq] - lam * o_t[:, tq:]).T
    a = a * lax.rsqrt(jnp.mean(a * a, axis=-1, keepdims=True) + LN_EPS) * g_ref[...]
    o_ref[...] = (a * (1.0 - lam_init)).astype(BF16)


def _diff_attn(q, k, vt, lq1, lk1, lq2, lk2, subln_g, lam_init, batch, seq):
    nq = seq // DIFF_TQ
    strips = (2 * DIFF_TQ // V7X_LANES, DIFF_TK, V7X_LANES)
    slopes = _alibi_slopes(DIFF_HEADS) * LOG2E
    vec = lambda v: v.reshape(1, -1).astype(F32)
    est = (2 * 2 * seq * DIFF_V_DIM * 2 + 4 * DIFF_TQ * DIFF_V_DIM * 2
           + DIFF_DEPTH * DIFF_TK * 2 * DIFF_TQ * (4 + 2)
           + 4 * DIFF_TK * 2 * DIFF_TQ * 4 + 2 * DIFF_V_DIM * 2 * DIFF_TQ * 4)
    small = lambda n: pl.BlockSpec((1, n), lambda b, h, i: (0, 0))
    return pl.pallas_call(
        functools.partial(_diff_attn_kernel, lam_init=lam_init, seq=seq),
        out_shape=jax.ShapeDtypeStruct((batch * seq, TOK_WIDTH), BF16),
        grid=(batch, DIFF_HEADS, nq),
        in_specs=[
            pl.BlockSpec(memory_space=pltpu.SMEM),
            pl.BlockSpec((None, None, DIFF_TQ, DIFF_V_DIM), lambda b, h, i: (b, h, i, 0)),
            pl.BlockSpec((None, None, seq, DIFF_V_DIM), lambda b, h, i: (b, h, 0, 0)),
            pl.BlockSpec((None, None, seq // DIFF_TK, DIFF_V_DIM, DIFF_TK), lambda b, h, i: (b, h, 0, 0, 0)),
            small(HEAD_DIM), small(HEAD_DIM), small(HEAD_DIM), small(HEAD_DIM), small(DIFF_V_DIM),
        ],
        out_specs=pl.BlockSpec((DIFF_TQ, DIFF_V_DIM), lambda b, h, i: (b * nq + i, h)),
        scratch_shapes=[
            pltpu.VMEM((2 * DIFF_V_DIM, 2 * DIFF_TQ), BF16),
            pltpu.VMEM((3, DIFF_TK, V7X_LANES), BF16),
            pltpu.VMEM((DIFF_TK, DIFF_TQ), F32),
            pltpu.VMEM((1, 2 * DIFF_TQ), F32),
            pltpu.VMEM((1, 2 * DIFF_TQ), F32),
            pltpu.VMEM((1, 2 * DIFF_TQ), F32),
            pltpu.VMEM((DIFF_V_DIM, 2 * DIFF_TQ), F32),
            pltpu.VMEM((DIFF_DEPTH,) + strips, F32),
            pltpu.VMEM((DIFF_DEPTH, 1, 2 * DIFF_TQ), F32),
            pltpu.VMEM((DIFF_DEPTH, 1, 2 * DIFF_TQ), F32),
            pltpu.VMEM((DIFF_DEPTH,) + strips, BF16),
            pltpu.VMEM((DIFF_DEPTH, 1, 2 * DIFF_TQ), F32),
        ],
        compiler_params=pltpu.CompilerParams(
            dimension_semantics=("arbitrary", "arbitrary", "arbitrary"), vmem_limit_bytes=_vmem_limit(est)),
        name="diff_attn",
    )(slopes, q, k, vt, vec(lq1), vec(lk1), vec(lq2), vec(lk2), vec(subln_g))


def _mem_kv_kernel(mem_ref, w_ref, kt_ref, v_ref):
    kv = jnp.dot(mem_ref[...].astype(BF16), w_ref[...], preferred_element_type=F32)
    kt = kv[:, :MEM_WIDTH].T
    v = kv[:, MEM_WIDTH:]
    pair_rows = 2 * HEAD_DIM
    row = lax.broadcasted_iota(jnp.int32, (pair_rows, N_MEM), 0)
    col = lax.broadcasted_iota(jnp.int32, (N_MEM, MEM_WIDTH), 1)
    for h in range(MEM_HEADS):
        pair = kt[(h // 2) * pair_rows:(h // 2 + 1) * pair_rows, :]
        keep = (row >= (h % 2) * HEAD_DIM) & (row < (h % 2 + 1) * HEAD_DIM)
        kt_ref[h] = jnp.where(keep, pair, jnp.zeros_like(pair)).astype(BF16)
        keep_v = (col >= h * HEAD_DIM) & (col < (h + 1) * HEAD_DIM)
        v_ref[h] = jnp.where(keep_v, v, jnp.zeros_like(v)).astype(BF16)


def _mem_kv(mem, w_mem_kv):
    batch = mem.shape[0]
    return pl.pallas_call(
        _mem_kv_kernel,
        out_shape=(
            jax.ShapeDtypeStruct((batch, MEM_HEADS, 2 * HEAD_DIM, N_MEM), BF16),
            jax.ShapeDtypeStruct((batch, MEM_HEADS, N_MEM, MEM_WIDTH), BF16),
        ),
        grid=(batch,),
        in_specs=[
            pl.BlockSpec((None, N_MEM, D_MODEL), lambda b: (b, 0, 0)),
            pl.BlockSpec((D_MODEL, 2 * MEM_WIDTH), lambda b: (0, 0)),
        ],
        out_specs=(
            pl.BlockSpec((None, MEM_HEADS, 2 * HEAD_DIM, N_MEM), lambda b: (b, 0, 0, 0)),
            pl.BlockSpec((None, MEM_HEADS, N_MEM, MEM_WIDTH), lambda b: (b, 0, 0, 0)),
        ),
        name="mem_kv",
    )(mem, w_mem_kv.astype(BF16))


def _mix_out_kernel(x_ref, o_ref, qm_ref, kt_ref, v_ref, wo_ref, wm_ref, g_ref, b_ref, out_ref):
    qm = qm_ref[...]
    pair_w = 2 * HEAD_DIM
    mem_o = None
    for h in range(MEM_HEADS):
        q_pair = qm[:, (h // 2) * pair_w:(h // 2 + 1) * pair_w]
        s = jnp.dot(q_pair, kt_ref[h], preferred_element_type=F32)
        e = jnp.exp(s - jnp.max(s, axis=-1, keepdims=True))
        p = (e / jnp.sum(e, axis=-1, keepdims=True)).astype(BF16)
        part = jnp.dot(p, v_ref[h], preferred_element_type=F32)
        mem_o = part if mem_o is None else mem_o + part
    y = jnp.dot(o_ref[...], wo_ref[...], preferred_element_type=F32)
    y = y + jnp.dot(mem_o.astype(BF16), wm_ref[...], preferred_element_type=F32)
    out_ref[...] = _layer_norm(DN_ALPHA * x_ref[...] + y, g_ref[...], b_ref[...])


def _mix_out(x, o, qm_arr, qm_col, kt, v, w_out, g, b, seq):
    n = x.shape[0]
    tiles_per_batch = seq // MIX_TM
    w_out = w_out.astype(BF16)
    est = (D_MODEL * D_MODEL * 2 + 4 * MIX_TM * D_MODEL * 4 + 2 * MIX_TM * D_MODEL * 2
           + 4 * MEM_HEADS * N_MEM * MEM_WIDTH * 2 + 6 * MIX_TM * D_MODEL * 4)
    return pl.pallas_call(
        _mix_out_kernel,
        out_shape=jax.ShapeDtypeStruct((n, D_MODEL), F32),
        grid=(n // MIX_TM,),
        in_specs=[
            pl.BlockSpec((MIX_TM, D_MODEL), lambda i: (i, 0)),
            pl.BlockSpec((MIX_TM, TOK_WIDTH), lambda i: (i, 0)),
            pl.BlockSpec((MIX_TM, MEM_WIDTH), lambda i: (i, qm_col)),
            pl.BlockSpec((None, MEM_HEADS, 2 * HEAD_DIM, N_MEM), lambda i: (i // tiles_per_batch, 0, 0, 0)),
            pl.BlockSpec((None, MEM_HEADS, N_MEM, MEM_WIDTH), lambda i: (i // tiles_per_batch, 0, 0, 0)),
            _resident((TOK_WIDTH, D_MODEL), lambda i: (0, 0)),
            _resident((MEM_WIDTH, D_MODEL), lambda i: (TOK_WIDTH // MEM_WIDTH, 0)),
            _resident((1, D_MODEL), lambda i: (0, 0)),
            _resident((1, D_MODEL), lambda i: (0, 0)),
        ],
        out_specs=pl.BlockSpec((MIX_TM, D_MODEL), lambda i: (i, 0)),
        compiler_params=pltpu.CompilerParams(
            dimension_semantics=("arbitrary",), vmem_limit_bytes=_vmem_limit(est)),
        name="mix_out",
    )(x, o, qm_arr, kt, v, w_out, w_out, g.reshape(1, D_MODEL), b.reshape(1, D_MODEL))


def _in_proj_b_kernel(x_ref, w_ref, q_ref, k_ref, vt_ref, qm_ref):
    assert WIN_KV_WIDTH == PROJ_CHUNK and MEM_WIDTH == PROJ_CHUNK
    xb = x_ref[...].astype(BF16)
    n_q = TOK_WIDTH // PROJ_CHUNK
    for c in range(B_IN // PROJ_CHUNK):
        cols = slice(c * PROJ_CHUNK, (c + 1) * PROJ_CHUNK)
        r = jnp.dot(xb, w_ref[:, cols], preferred_element_type=F32)
        if c < n_q:
            q_ref[:, cols] = (r * (QK_SCALE * LOG2E)).astype(BF16)
        elif c == n_q:
            k_ref[...] = r.astype(BF16)
        elif c == n_q + 1:
            rt = r.T.astype(BF16)
            for t in range(PROJ_TM // WINDOW):
                vt_ref[t] = rt[:, t * WINDOW:(t + 1) * WINDOW]
        else:
            qm_ref[...] = (r * QK_SCALE).astype(BF16)


def _in_proj_b(x, w_in, batch, seq):
    n = x.shape[0]
    tiles_per_batch = seq // PROJ_TM
    est = D_MODEL * B_IN * 2 + 2 * PROJ_TM * D_MODEL * 4 + 2 * PROJ_TM * B_IN * 2 + 4 * PROJ_TM * PROJ_CHUNK * 4
    return pl.pallas_call(
        _in_proj_b_kernel,
        out_shape=(
            jax.ShapeDtypeStruct((n, TOK_WIDTH), BF16),
            jax.ShapeDtypeStruct((n, WIN_KV_WIDTH), BF16),
            jax.ShapeDtypeStruct((batch, seq // WINDOW, WIN_KV_WIDTH, WINDOW), BF16),
            jax.ShapeDtypeStruct((n, MEM_WIDTH), BF16),
        ),
        grid=(n // PROJ_TM,),
        in_specs=[
            pl.BlockSpec((PROJ_TM, D_MODEL), lambda i: (i, 0)),
            _resident((D_MODEL, B_IN), lambda i: (0, 0)),
        ],
        out_specs=(
            pl.BlockSpec((PROJ_TM, TOK_WIDTH), lambda i: (i, 0)),
            pl.BlockSpec((PROJ_TM, WIN_KV_WIDTH), lambda i: (i, 0)),
            pl.BlockSpec((None, PROJ_TM // WINDOW, WIN_KV_WIDTH, WINDOW),
                         lambda i: (i // tiles_per_batch, i % tiles_per_batch, 0, 0)),
            pl.BlockSpec((PROJ_TM, MEM_WIDTH), lambda i: (i, 0)),
        ),
        compiler_params=pltpu.CompilerParams(
            dimension_semantics=("arbitrary",), vmem_limit_bytes=_vmem_limit(est)),
        name="in_proj_b",
    )(x, w_in.astype(BF16))


def _win_attn_kernel(slopes_ref, sink_ref, q_ref, k_ref, vt_ref, o_ref,
                     rel_ref, w_ref, st_ref, p_ref, linv_ref, ot_ref, *, seq):
    i = pl.program_id(1)
    span = 3 * WINDOW
    nblk = WIN_TQ // WINDOW
    pair = 2 * HEAD_DIM
    assert WINDOW == V7X_LANES and nblk >= 4

    rel_ref[...] = (lax.broadcasted_iota(jnp.int32, (span, WINDOW), 0)
                    - lax.broadcasted_iota(jnp.int32, (span, WINDOW), 1)).astype(F32)

    def rows_of(n):
        return pl.ds(n * WINDOW, WINDOW) if isinstance(n, int) else pl.ds(pl.multiple_of(n * WINDOW, WINDOW), WINDOW)

    def window_of(n):
        gn = i * nblk + n
        return jnp.clip(gn - 1, 0, seq // WINDOW - 3), gn

    def prep_q(n):
        qt = q_ref[rows_of(n), :].astype(F32).T
        zeros = jnp.zeros((HEAD_DIM, WINDOW), F32)
        for g in range(WIN_KV_HEADS):
            cols = []
            for hh in range(WIN_GROUP):
                x = qt[(g * WIN_GROUP + hh) * HEAD_DIM:(g * WIN_GROUP + hh + 1) * HEAD_DIM, :]
                cols.append(jnp.concatenate([x, zeros] if g % 2 == 0 else [zeros, x], axis=0))
            w_ref[g] = jnp.concatenate(cols, axis=1).astype(BF16)

    def scores(n, g):
        c0, _ = window_of(n)
        keys = k_ref[pl.ds(pl.multiple_of(c0 * WINDOW, WINDOW), span), (g // 2) * pair:(g // 2 + 1) * pair]
        st_ref[g] = jnp.dot(keys, w_ref[g], preferred_element_type=F32)

    def softmax(n, g):
        c0, gn = window_of(n)
        dist = jnp.abs(((gn - c0) * WINDOW).astype(F32) - rel_ref[...])
        penalty = jnp.where(dist <= WINDOW, 0.0, -NEG)
        for hh in range(WIN_GROUP):
            head = g * WIN_GROUP + hh
            lanes = slice(hh * WINDOW, (hh + 1) * WINDOW)
            s = st_ref[g, :, lanes] - (slopes_ref[head] * dist + penalty)
            sink = sink_ref[head]
            m = jnp.maximum(jnp.max(s, axis=0, keepdims=True), sink)
            p = jnp.exp2(s - m)
            p_ref[g, :, lanes] = p.astype(BF16)
            linv_ref[g, :, lanes] = 1.0 / (jnp.sum(p, axis=0, keepdims=True) + jnp.exp2(sink - m))

    def pv(n, g):
        c0, _ = window_of(n)
        vt = jnp.concatenate([vt_ref[c0 + e, g * HEAD_DIM:(g + 1) * HEAD_DIM, :] for e in range(3)], axis=1)
        ot_ref[g] = jnp.dot(vt, p_ref[g], preferred_element_type=F32) * linv_ref[g]

    def assemble(n):
        for pp in range(WIN_Q_HEADS // 2):
            parts = []
            for head in (2 * pp, 2 * pp + 1):
                g, hh = divmod(head, WIN_GROUP)
                parts.append(ot_ref[g, :, hh * WINDOW:(hh + 1) * WINDOW])
            o_ref[rows_of(n), pp * pair:(pp + 1) * pair] = jnp.concatenate(parts, axis=0).T.astype(BF16)

    def iteration(n, do_assemble, do_pv, do_softmax, do_scores):
        if do_assemble:
            assemble(n - 2)
        if do_scores:
            prep_q(n + 1)
        for g in range(WIN_KV_HEADS):
            if do_pv:
                pv(n - 1, g)
            if do_softmax:
                softmax(n, g)
            if do_scores:
                scores(n + 1, g)

    prep_q(0)
    for g in range(WIN_KV_HEADS):
        scores(0, g)
    iteration(0, False, False, True, True)
    iteration(1, False, True, True, True)

    def body(n, carry):
        iteration(n, True, True, True, True)
        return carry

    lax.fori_loop(2, nblk - 1, body, 0)
    iteration(nblk - 1, True, True, True, False)
    iteration(nblk, True, True, False, False)
    iteration(nblk + 1, True, False, False, False)


def _win_attn(q, k, vt, sink, batch, seq):
    nq = seq // WIN_TQ
    span = 3 * WINDOW
    slopes = _alibi_slopes(WIN_Q_HEADS) * LOG2E
    est = (2 * 2 * seq * WIN_KV_WIDTH * 2 + 4 * WIN_TQ * TOK_WIDTH * 2
           + WIN_KV_HEADS * span * span * (4 + 2) + 8 * span * span * 4)
    return pl.pallas_call(
        functools.partial(_win_attn_kernel, seq=seq),
        out_shape=jax.ShapeDtypeStruct((batch * seq, TOK_WIDTH), BF16),
        grid=(batch, nq),
        in_specs=[
            pl.BlockSpec(memory_space=pltpu.SMEM),
            pl.BlockSpec(memory_space=pltpu.SMEM),
            pl.BlockSpec((WIN_TQ, TOK_WIDTH), lambda b, i: (b * nq + i, 0)),
            pl.BlockSpec((seq, WIN_KV_WIDTH), lambda b, i: (b, 0)),
            pl.BlockSpec((None, seq // WINDOW, WIN_KV_WIDTH, WINDOW), lambda b, i: (b, 0, 0, 0)),
        ],
        out_specs=pl.BlockSpec((WIN_TQ, TOK_WIDTH), lambda b, i: (b * nq + i, 0)),
        scratch_shapes=[
            pltpu.VMEM((span, WINDOW), F32),
            pltpu.VMEM((WIN_KV_HEADS, 2 * HEAD_DIM, span), BF16),
            pltpu.VMEM((WIN_KV_HEADS, span, span), F32),
            pltpu.VMEM((WIN_KV_HEADS, span, span), BF16),
            pltpu.VMEM((WIN_KV_HEADS, 1, span), F32),
            pltpu.VMEM((WIN_KV_HEADS, HEAD_DIM, span), F32),
        ],
        compiler_params=pltpu.CompilerParams(
            dimension_semantics=("arbitrary", "arbitrary"), vmem_limit_bytes=_vmem_limit(est)),
        name="win_attn",
    )(slopes, sink.astype(F32) * LOG2E, q, k, vt)


def kernel(x, mem, ffn1_w13, ffn1_w2, ln1_g, ln1_b, w_mem_kv, w_out, ln2_g, ln2_b, ffn2_w13, ffn2_w2,
           ln3_g, ln3_b, a_w_in, a_lambda_q1, a_lambda_k1, a_lambda_q2, a_lambda_k2, a_subln_g, b_w_in, b_sink):
    batch, seq, _ = x.shape
    xf = x.reshape(batch * seq, D_MODEL)
    for i in range(DEPTH):
        xf = _ffn_ln(xf, ffn1_w13[i], ffn1_w2[i], ln1_g[i], ln1_b[i])
        j = i // N_MIXERS
        kt, v = _mem_kv(mem, w_mem_kv[i])
        if i % N_MIXERS == 0:
            lam_init = 0.8 - 0.6 * math.exp(-0.3 * i)
            q, k, vt, qm = _in_proj_a(xf, a_w_in[j], batch, seq)
            o = _diff_attn(q, k, vt, a_lambda_q1[j], a_lambda_k1[j], a_lambda_q2[j], a_lambda_k2[j],
                           a_subln_g[j], lam_init, batch, seq)
            xf = _mix_out(xf, o, qm, 0, kt, v, w_out[i], ln2_g[i], ln2_b[i], seq)
        else:
            q, k, vt, qm = _in_proj_b(xf, b_w_in[j], batch, seq)
            o = _win_attn(q, k, vt, b_sink[j], batch, seq)
            xf = _mix_out(xf, o, qm, 0, kt, v, w_out[i], ln2_g[i], ln2_b[i], seq)
        xf = _ffn_ln(xf, ffn2_w13[i], ffn2_w2[i], ln3_g[i], ln3_b[i])
    return xf.reshape(batch, seq, D_MODEL)
```

```python
import functools
import math

import numpy as np
import jax
import jax.numpy as jnp
from jax import lax
from jax.experimental import pallas as pl
from jax.experimental.pallas import tpu as pltpu

D_MODEL = 1024
DEPTH = 2
N_MIXERS = 2
N_MEM = 256
HEAD_DIM = 64
MEM_HEADS = 4
MEM_WIDTH = MEM_HEADS * HEAD_DIM
TOK_WIDTH = D_MODEL - MEM_WIDTH
DIFF_HEADS = TOK_WIDTH // (2 * HEAD_DIM)
DIFF_V_DIM = 2 * HEAD_DIM
QK_A = DIFF_HEADS * 2 * HEAD_DIM
WIN_Q_HEADS = TOK_WIDTH // HEAD_DIM
WIN_GROUP = 3
WIN_KV_HEADS = WIN_Q_HEADS // WIN_GROUP
WIN_KV_WIDTH = WIN_KV_HEADS * HEAD_DIM
WINDOW = 128
D_FF = 2816
A_IN = 2 * QK_A + TOK_WIDTH + MEM_WIDTH
B_IN = TOK_WIDTH + 2 * WIN_KV_WIDTH + MEM_WIDTH
N_A = (DEPTH + 1) // 2
DN_ALPHA = (2 * DEPTH) ** 0.25
LN_EPS = 1e-5
NEG = -1e30
QK_SCALE = HEAD_DIM ** -0.5
LOG2E = math.log2(math.e)

F32 = jnp.float32
BF16 = jnp.bfloat16

V7X_LANES = 128
V7X_MXU_DIM = 256
V7X_VMEM_BYTES = 64 * 1024 * 1024

FFN_TM = 512
FF_CHUNK = V7X_MXU_DIM
PROJ_TM = 1024
PROJ_CHUNK = V7X_MXU_DIM
DIFF_TQ = 256
DIFF_TK = 256
DIFF_DEPTH = 8
MIX_TM = 512
WIN_TQ = 1024


def _vmem_limit(nbytes):
    return int(min(V7X_VMEM_BYTES - 8 * 1024 * 1024, nbytes * 3 // 2))


def _resident(block_shape, index_map):
    return pl.BlockSpec(block_shape, index_map, pipeline_mode=pl.Buffered(1))


def _layer_norm(y, g, b):
    mu = jnp.mean(y, axis=-1, keepdims=True)
    yc = y - mu
    var = jnp.mean(yc * yc, axis=-1, keepdims=True)
    return yc * lax.rsqrt(var + LN_EPS) * g + b


def _alibi_slopes(n):
    return jnp.asarray(2.0 ** (-8.0 * np.arange(1, n + 1) / n), dtype=F32)


def _ffn_ln_kernel(x_ref, wg_ref, wu_ref, w2_ref, g_ref, b_ref, o_ref, acc_ref):
    x = x_ref[...]
    xb = x.astype(BF16)
    for c in range(D_FF // FF_CHUNK):
        cols = slice(c * FF_CHUNK, (c + 1) * FF_CHUNK)
        gate = jnp.dot(xb, wg_ref[:, cols], preferred_element_type=F32)
        up = jnp.dot(xb, wu_ref[:, cols], preferred_element_type=F32)
        act = (gate / (1.0 + jnp.exp(-gate)) * up).astype(BF16)
        part = jnp.dot(act, w2_ref[cols, :], preferred_element_type=F32)
        if c == 0:
            acc_ref[...] = part
        else:
            acc_ref[...] += part
    y = DN_ALPHA * x + 0.5 * acc_ref[...]
    o_ref[...] = _layer_norm(y, g_ref[...], b_ref[...])


def _ffn_ln(x, w13, w2, g, b):
    n = x.shape[0]
    w13 = w13.astype(BF16)
    w2 = w2.astype(BF16)
    est = (2 * D_MODEL * D_FF + D_FF * D_MODEL) * 2 + 5 * FFN_TM * D_MODEL * 4 + 4 * FFN_TM * FF_CHUNK * 4
    return pl.pallas_call(
        _ffn_ln_kernel,
        out_shape=jax.ShapeDtypeStruct((n, D_MODEL), F32),
        grid=(n // FFN_TM,),
        in_specs=[
            pl.BlockSpec((FFN_TM, D_MODEL), lambda i: (i, 0)),
            _resident((D_MODEL, D_FF), lambda i: (0, 0)),
            _resident((D_MODEL, D_FF), lambda i: (0, 1)),
            _resident((D_FF, D_MODEL), lambda i: (0, 0)),
            _resident((1, D_MODEL), lambda i: (0, 0)),
            _resident((1, D_MODEL), lambda i: (0, 0)),
        ],
        out_specs=pl.BlockSpec((FFN_TM, D_MODEL), lambda i: (i, 0)),
        scratch_shapes=[pltpu.VMEM((FFN_TM, D_MODEL), F32)],
        compiler_params=pltpu.CompilerParams(
            dimension_semantics=("arbitrary",), vmem_limit_bytes=_vmem_limit(est)),
        name="ffn_ln",
    )(x, w13, w13, w2, g.reshape(1, D_MODEL), b.reshape(1, D_MODEL))


def _in_proj_a_kernel(x_ref, w_ref, q_ref, k_ref, vt_ref, qm_ref):
    xb = x_ref[...].astype(BF16)
    heads_per_chunk = PROJ_CHUNK // DIFF_V_DIM
    n_q = QK_A // PROJ_CHUNK
    n_qk = 2 * n_q
    n_v = TOK_WIDTH // PROJ_CHUNK
    for c in range(A_IN // PROJ_CHUNK):
        r = jnp.dot(xb, w_ref[:, c * PROJ_CHUNK:(c + 1) * PROJ_CHUNK], preferred_element_type=F32)
        if c < n_qk:
            dst, c0, scale = (q_ref, 0, QK_SCALE * LOG2E) if c < n_q else (k_ref, n_q, 1.0)
            for hh in range(heads_per_chunk):
                dst[(c - c0) * heads_per_chunk + hh] = (
                    r[:, hh * DIFF_V_DIM:(hh + 1) * DIFF_V_DIM] * scale).astype(BF16)
        elif c < n_qk + n_v:
            rt = r.T.astype(BF16)
            for hh in range(heads_per_chunk):
                for t in range(PROJ_TM // DIFF_TK):
                    vt_ref[(c - n_qk) * heads_per_chunk + hh, t] = rt[
                        hh * DIFF_V_DIM:(hh + 1) * DIFF_V_DIM, t * DIFF_TK:(t + 1) * DIFF_TK]
        else:
            qm_ref[...] = (r * QK_SCALE).astype(BF16)


def _in_proj_a(x, w_in, batch, seq):
    n = x.shape[0]
    w_in = w_in.astype(BF16)
    tiles_per_batch = seq // PROJ_TM
    chunks_per_tile = PROJ_TM // DIFF_TK
    est = D_MODEL * A_IN * 2 + 2 * PROJ_TM * D_MODEL * 4 + 2 * PROJ_TM * A_IN * 2 + 4 * PROJ_TM * PROJ_CHUNK * 4
    head_major = jax.ShapeDtypeStruct((batch, DIFF_HEADS, seq, DIFF_V_DIM), BF16)
    head_major_spec = pl.BlockSpec((None, DIFF_HEADS, PROJ_TM, DIFF_V_DIM),
                                   lambda i: (i // tiles_per_batch, 0, i % tiles_per_batch, 0))
    return pl.pallas_call(
        _in_proj_a_kernel,
        out_shape=(
            head_major,
            head_major,
            jax.ShapeDtypeStruct((batch, DIFF_HEADS, seq // DIFF_TK, DIFF_V_DIM, DIFF_TK), BF16),
            jax.ShapeDtypeStruct((n, MEM_WIDTH), BF16),
        ),
        grid=(n // PROJ_TM,),
        in_specs=[
            pl.BlockSpec((PROJ_TM, D_MODEL), lambda i: (i, 0)),
            _resident((D_MODEL, A_IN), lambda i: (0, 0)),
        ],
        out_specs=(
            head_major_spec,
            head_major_spec,
            pl.BlockSpec((None, DIFF_HEADS, chunks_per_tile, DIFF_V_DIM, DIFF_TK),
                         lambda i: (i // tiles_per_batch, 0, i % tiles_per_batch, 0, 0)),
            pl.BlockSpec((PROJ_TM, MEM_WIDTH), lambda i: (i, 0)),
        ),
        compiler_params=pltpu.CompilerParams(
            dimension_semantics=("arbitrary",), vmem_limit_bytes=_vmem_limit(est)),
        name="in_proj_a",
    )(x, w_in)


def _diff_attn_kernel(slopes_ref, q_ref, k_ref, vt_ref, lq1_ref, lk1_ref, lq2_ref, lk2_ref, g_ref,
                      o_ref, qt_ref, pos_ref, bias_ref, sr_ref, m_ref, l_ref, acc_ref,
                      st_ref, shift_ref, max_ref, p_ref, alpha_ref, *, lam_init, seq):
    h = pl.program_id(1)
    i = pl.program_id(2)
    tq, tk = DIFF_TQ, DIFF_TK
    assert tq == tk and tk <= 256
    n_steps = seq // tk
    n_strips = 2 * tq // V7X_LANES
    slope = slopes_ref[h]

    qt = q_ref[...].astype(F32).T
    row = lax.broadcasted_iota(jnp.int32, qt.shape, 0)
    zero = jnp.zeros_like(qt)
    qt_ref[0:DIFF_V_DIM, :tq] = jnp.where(row < HEAD_DIM, qt, zero).astype(BF16)
    qt_ref[0:DIFF_V_DIM, tq:] = jnp.where(row >= HEAD_DIM, qt, zero).astype(BF16)
    slope_v = jnp.full((16, 2 * tq), slope, F32)
    s_hi = slope_v.astype(BF16).astype(F32)
    s_mid = (slope_v - s_hi).astype(BF16).astype(F32)
    s_lo = slope_v - s_hi - s_mid
    prow = lax.broadcasted_iota(jnp.int32, (16, 2 * tq), 0)
    pieces = jnp.where(prow == 0, s_hi, jnp.where(prow == 1, s_mid, jnp.where(prow == 2, s_lo, 0.0)))
    qt_ref[DIFF_V_DIM:DIFF_V_DIM + 16, :] = pieces.astype(BF16)
    qt_ref[DIFF_V_DIM + 16:, :] = jnp.zeros((DIFF_V_DIM - 16, 2 * tq), BF16)

    r_s = lax.broadcasted_iota(jnp.int32, (tk, V7X_LANES), 0).astype(F32)
    lane = lax.broadcasted_iota(jnp.int32, (tk, V7X_LANES), 1)
    r_s = jnp.where(lane < 3, r_s, 0.0)
    pos_ref[0] = r_s.astype(BF16)
    pos_ref[1] = (-r_s).astype(BF16)
    pos_ref[2] = jnp.zeros((tk, V7X_LANES), BF16)

    rel = (lax.broadcasted_iota(jnp.int32, (tk, tq), 0) - lax.broadcasted_iota(jnp.int32, (tk, tq), 1))
    bias_ref[...] = slope * jnp.abs(rel).astype(F32)
    r_t = lax.broadcasted_iota(jnp.int32, (1, 2 * tq), 1)
    sr_ref[...] = slope * jnp.where(r_t >= tq, r_t - tq, r_t).astype(F32)

    m_ref[...] = jnp.full(m_ref.shape, -jnp.inf, F32)
    l_ref[...] = jnp.zeros(l_ref.shape, F32)
    acc_ref[...] = jnp.zeros(acc_ref.shape, F32)

    def keys(j):
        return k_ref[pl.ds(pl.multiple_of(j * tk, tk), tk), :]

    def tile_of(t):
        return jnp.where(t == 0, i, jnp.where(t > i, t, t - 1))

    def strip(c):
        return slice(c * V7X_LANES, (c + 1) * V7X_LANES)

    def scores(t, b):
        j = tile_of(t)
        diag = isinstance(t, int) and t == 0
        if diag:
            k_aug = jnp.concatenate([keys(j), pos_ref[2]], axis=1)
            shift_ref[b] = jnp.zeros((1, 2 * tq), F32)
        else:
            after = (j > i).astype(jnp.int32)
            k_aug = jnp.concatenate([keys(j), pos_ref[after]], axis=1)
            sign = (1 - 2 * after).astype(F32)
            tile_dist = (jnp.abs(j - i) * tk).astype(F32)
            shift_ref[b] = -(sign * sr_ref[...]) - slope * tile_dist
        st = jnp.dot(k_aug, qt_ref[...], preferred_element_type=F32)
        for c in range(n_strips):
            blk = st[:, strip(c)]
            if diag:
                blk = blk - bias_ref[:, strip(c % (tq // V7X_LANES))]
            st_ref[b, c] = blk
            max_ref[b, :, strip(c)] = jnp.max(blk, axis=0, keepdims=True)

    def softmax(b):
        for c in range(n_strips):
            lanes = strip(c)
            shift = shift_ref[b, :, lanes]
            m_old = m_ref[:, lanes]
            m_new = jnp.maximum(m_old, max_ref[b, :, lanes] + shift)
            alpha = jnp.exp2(m_old - m_new)
            p = jnp.exp2(st_ref[b, c] - (m_new - shift))
            l_ref[:, lanes] = alpha * l_ref[:, lanes] + jnp.sum(p, axis=0, keepdims=True)
            m_ref[:, lanes] = m_new
            p_ref[b, c] = p.astype(BF16)
            alpha_ref[b, :, lanes] = alpha

    def pv(t, b):
        p = jnp.concatenate([p_ref[b, c] for c in range(n_strips)], axis=1)
        acc_ref[...] = alpha_ref[b] * acc_ref[...] + jnp.dot(vt_ref[tile_of(t)], p,
                                                              preferred_element_type=F32)

    depth = DIFF_DEPTH
    assert (n_steps - 2 * depth) % depth == 0

    def stage(t, r):
        if not (isinstance(t, int) and t < depth):
            pv(t - depth, r)
        softmax(r)
        if not (isinstance(t, int) and t + depth >= n_steps):
            scores(t + depth, r)

    for t in range(depth):
        scores(t, t)
    for t in range(depth):
        stage(t, t)

    def body(k, carry):
        for r in range(depth):
            stage(depth * (k + 1) + r, r)
        return carry

    lax.fori_loop(0, (n_steps - 2 * depth) // depth, body, 0)
    for t in range(n_steps - depth, n_steps):
        stage(t, t % depth)
    for t in range(n_steps - depth, n_steps):
        pv(t, t % depth)

    lam = (jnp.exp(jnp.sum(lq1_ref[...] * lk1_ref[...], axis=-1, keepdims=True))
           - jnp.exp(jnp.sum(lq2_ref[...] * lk2_ref[...], axis=-1, keepdims=True)) + lam_init)
    o_t = acc_ref[...] / l_ref[...]
    a = (o_t[:, :tq] - lam * o_t[:, tq:]).T
    a = a * lax.rsqrt(jnp.mean(a * a, axis=-1, keepdims=True) + LN_EPS) * g_ref[...]
    o_ref[...] = (a * (1.0 - lam_init)).astype(BF16)


def _diff_attn(q, k, vt, lq1, lk1, lq2, lk2, subln_g, lam_init, batch, seq):
    nq = seq // DIFF_TQ
    strips = (2 * DIFF_TQ // V7X_LANES, DIFF_TK, V7X_LANES)
    slopes = _alibi_slopes(DIFF_HEADS) * LOG2E
    vec = lambda v: v.reshape(1, -1).astype(F32)
    est = (2 * 2 * seq * DIFF_V_DIM * 2 + 4 * DIFF_TQ * DIFF_V_DIM * 2
           + DIFF_DEPTH * DIFF_TK * 2 * DIFF_TQ * (4 + 2)
           + 4 * DIFF_TK * 2 * DIFF_TQ * 4 + 2 * DIFF_V_DIM * 2 * DIFF_TQ * 4)
    small = lambda n: pl.BlockSpec((1, n), lambda b, h, i: (0, 0))
    return pl.pallas_call(
        functools.partial(_diff_attn_kernel, lam_init=lam_init, seq=seq),
        out_shape=jax.ShapeDtypeStruct((batch * seq, TOK_WIDTH), BF16),
        grid=(batch, DIFF_HEADS, nq),
        in_specs=[
            pl.BlockSpec(memory_space=pltpu.SMEM),
            pl.BlockSpec((None, None, DIFF_TQ, DIFF_V_DIM), lambda b, h, i: (b, h, i, 0)),
            pl.BlockSpec((None, None, seq, DIFF_V_DIM), lambda b, h, i: (b, h, 0, 0)),
            pl.BlockSpec((None, None, seq // DIFF_TK, DIFF_V_DIM, DIFF_TK), lambda b, h, i: (b, h, 0, 0, 0)),
            small(HEAD_DIM), small(HEAD_DIM), small(HEAD_DIM), small(HEAD_DIM), small(DIFF_V_DIM),
        ],
        out_specs=pl.BlockSpec((DIFF_TQ, DIFF_V_DIM), lambda b, h, i: (b * nq + i, h)),
        scratch_shapes=[
            pltpu.VMEM((2 * DIFF_V_DIM, 2 * DIFF_TQ), BF16),
            pltpu.VMEM((3, DIFF_TK, V7X_LANES), BF16),
            pltpu.VMEM((DIFF_TK, DIFF_TQ), F32),
            pltpu.VMEM((1, 2 * DIFF_TQ), F32),
            pltpu.VMEM((1, 2 * DIFF_TQ), F32),
            pltpu.VMEM((1, 2 * DIFF_TQ), F32),
            pltpu.VMEM((DIFF_V_DIM, 2 * DIFF_TQ), F32),
            pltpu.VMEM((DIFF_DEPTH,) + strips, F32),
            pltpu.VMEM((DIFF_DEPTH, 1, 2 * DIFF_TQ), F32),
            pltpu.VMEM((DIFF_DEPTH, 1, 2 * DIFF_TQ), F32),
            pltpu.VMEM((DIFF_DEPTH,) + strips, BF16),
            pltpu.VMEM((DIFF_DEPTH, 1, 2 * DIFF_TQ), F32),
        ],
        compiler_params=pltpu.CompilerParams(
            dimension_semantics=("arbitrary", "arbitrary", "arbitrary"), vmem_limit_bytes=_vmem_limit(est)),
        name="diff_attn",
    )(slopes, q, k, vt, vec(lq1), vec(lk1), vec(lq2), vec(lk2), vec(subln_g))


def _mem_kv_kernel(mem_ref, w_ref, kt_ref, v_ref):
    kv = jnp.dot(mem_ref[...].astype(BF16), w_ref[...], preferred_element_type=F32)
    kt = kv[:, :MEM_WIDTH].T
    v = kv[:, MEM_WIDTH:]
    pair_rows = 2 * HEAD_DIM
    row = lax.broadcasted_iota(jnp.int32, (pair_rows, N_MEM), 0)
    col = lax.broadcasted_iota(jnp.int32, (N_MEM, MEM_WIDTH), 1)
    for h in range(MEM_HEADS):
        pair = kt[(h // 2) * pair_rows:(h // 2 + 1) * pair_rows, :]
        keep = (row >= (h % 2) * HEAD_DIM) & (row < (h % 2 + 1) * HEAD_DIM)
        kt_ref[h] = jnp.where(keep, pair, jnp.zeros_like(pair)).astype(BF16)
        keep_v = (col >= h * HEAD_DIM) & (col < (h + 1) * HEAD_DIM)
        v_ref[h] = jnp.where(keep_v, v, jnp.zeros_like(v)).astype(BF16)


def _mem_kv(mem, w_mem_kv):
    batch = mem.shape[0]
    return pl.pallas_call(
        _mem_kv_kernel,
        out_shape=(
            jax.ShapeDtypeStruct((batch, MEM_HEADS, 2 * HEAD_DIM, N_MEM), BF16),
            jax.ShapeDtypeStruct((batch, MEM_HEADS, N_MEM, MEM_WIDTH), BF16),
        ),
        grid=(batch,),
        in_specs=[
            pl.BlockSpec((None, N_MEM, D_MODEL), lambda b: (b, 0, 0)),
            pl.BlockSpec((D_MODEL, 2 * MEM_WIDTH), lambda b: (0, 0)),
        ],
        out_specs=(
            pl.BlockSpec((None, MEM_HEADS, 2 * HEAD_DIM, N_MEM), lambda b: (b, 0, 0, 0)),
            pl.BlockSpec((None, MEM_HEADS, N_MEM, MEM_WIDTH), lambda b: (b, 0, 0, 0)),
        ),
        name="mem_kv",
    )(mem, w_mem_kv.astype(BF16))


def _mix_out_kernel(x_ref, o_ref, qm_ref, kt_ref, v_ref, wo_ref, wm_ref, g_ref, b_ref, out_ref):
    qm = qm_ref[...]
    pair_w = 2 * HEAD_DIM
    mem_o = None
    for h in range(MEM_HEADS):
        q_pair = qm[:, (h // 2) * pair_w:(h // 2 + 1) * pair_w]
        s = jnp.dot(q_pair, kt_ref[h], preferred_element_type=F32)
        e = jnp.exp(s - jnp.max(s, axis=-1, keepdims=True))
        p = (e / jnp.sum(e, axis=-1, keepdims=True)).astype(BF16)
        part = jnp.dot(p, v_ref[h], preferred_element_type=F32)
        mem_o = part if mem_o is None else mem_o + part
    y = jnp.dot(o_ref[...], wo_ref[...], preferred_element_type=F32)
    y = y + jnp.dot(mem_o.astype(BF16), wm_ref[...], preferred_element_type=F32)
    out_ref[...] = _layer_norm(DN_ALPHA * x_ref[...] + y, g_ref[...], b_ref[...])


def _mix_out(x, o, qm_arr, qm_col, kt, v, w_out, g, b, seq):
    n = x.shape[0]
    tiles_per_batch = seq // MIX_TM
    w_out = w_out.astype(BF16)
    est = (D_MODEL * D_MODEL * 2 + 4 * MIX_TM * D_MODEL * 4 + 2 * MIX_TM * D_MODEL * 2
           + 4 * MEM_HEADS * N_MEM * MEM_WIDTH * 2 + 6 * MIX_TM * D_MODEL * 4)
    return pl.pallas_call(
        _mix_out_kernel,
        out_shape=jax.ShapeDtypeStruct((n, D_MODEL), F32),
        grid=(n // MIX_TM,),
        in_specs=[
            pl.BlockSpec((MIX_TM, D_MODEL), lambda i: (i, 0)),
            pl.BlockSpec((MIX_TM, TOK_WIDTH), lambda i: (i, 0)),
            pl.BlockSpec((MIX_TM, MEM_WIDTH), lambda i: (i, qm_col)),
            pl.BlockSpec((None, MEM_HEADS, 2 * HEAD_DIM, N_MEM), lambda i: (i // tiles_per_batch, 0, 0, 0)),
            pl.BlockSpec((None, MEM_HEADS, N_MEM, MEM_WIDTH), lambda i: (i // tiles_per_batch, 0, 0, 0)),
            _resident((TOK_WIDTH, D_MODEL), lambda i: (0, 0)),
            _resident((MEM_WIDTH, D_MODEL), lambda i: (TOK_WIDTH // MEM_WIDTH, 0)),
            _resident((1, D_MODEL), lambda i: (0, 0)),
            _resident((1, D_MODEL), lambda i: (0, 0)),
        ],
        out_specs=pl.BlockSpec((MIX_TM, D_MODEL), lambda i: (i, 0)),
        compiler_params=pltpu.CompilerParams(
            dimension_semantics=("arbitrary",), vmem_limit_bytes=_vmem_limit(est)),
        name="mix_out",
    )(x, o, qm_arr, kt, v, w_out, w_out, g.reshape(1, D_MODEL), b.reshape(1, D_MODEL))


def _in_proj_b_kernel(x_ref, w_ref, q_ref, k_ref, vt_ref, qm_ref):
    assert WIN_KV_WIDTH == PROJ_CHUNK and MEM_WIDTH == PROJ_CHUNK
    xb = x_ref[...].astype(BF16)
    n_q = TOK_WIDTH // PROJ_CHUNK
    for c in range(B_IN // PROJ_CHUNK):
        cols = slice(c * PROJ_CHUNK, (c + 1) * PROJ_CHUNK)
        r = jnp.dot(xb, w_ref[:, cols], preferred_element_type=F32)
        if c < n_q:
            q_ref[:, cols] = (r * (QK_SCALE * LOG2E)).astype(BF16)
        elif c == n_q:
            k_ref[...] = r.astype(BF16)
        elif c == n_q + 1:
            rt = r.T.astype(BF16)
            for t in range(PROJ_TM // WINDOW):
                vt_ref[t] = rt[:, t * WINDOW:(t + 1) * WINDOW]
        else:
            qm_ref[...] = (r * QK_SCALE).astype(BF16)


def _in_proj_b(x, w_in, batch, seq):
    n = x.shape[0]
    tiles_per_batch = seq // PROJ_TM
    est = D_MODEL * B_IN * 2 + 2 * PROJ_TM * D_MODEL * 4 + 2 * PROJ_TM * B_IN * 2 + 4 * PROJ_TM * PROJ_CHUNK * 4
    return pl.pallas_call(
        _in_proj_b_kernel,
        out_shape=(
            jax.ShapeDtypeStruct((n, TOK_WIDTH), BF16),
            jax.ShapeDtypeStruct((n, WIN_KV_WIDTH), BF16),
            jax.ShapeDtypeStruct((batch, seq // WINDOW, WIN_KV_WIDTH, WINDOW), BF16),
            jax.ShapeDtypeStruct((n, MEM_WIDTH), BF16),
        ),
        grid=(n // PROJ_TM,),
        in_specs=[
            pl.BlockSpec((PROJ_TM, D_MODEL), lambda i: (i, 0)),
            _resident((D_MODEL, B_IN), lambda i: (0, 0)),
        ],
        out_specs=(
            pl.BlockSpec((PROJ_TM, TOK_WIDTH), lambda i: (i, 0)),
            pl.BlockSpec((PROJ_TM, WIN_KV_WIDTH), lambda i: (i, 0)),
            pl.BlockSpec((None, PROJ_TM // WINDOW, WIN_KV_WIDTH, WINDOW),
                         lambda i: (i // tiles_per_batch, i % tiles_per_batch, 0, 0)),
            pl.BlockSpec((PROJ_TM, MEM_WIDTH), lambda i: (i, 0)),
        ),
        compiler_params=pltpu.CompilerParams(
            dimension_semantics=("arbitrary",), vmem_limit_bytes=_vmem_limit(est)),
        name="in_proj_b",
    )(x, w_in.astype(BF16))


def _win_attn_kernel(slopes_ref, sink_ref, q_ref, k_ref, vt_ref, o_ref,
                     rel_ref, w_ref, st_ref, p_ref, linv_ref, ot_ref, *, seq):
    i = pl.program_id(1)
    span = 3 * WINDOW
    nblk = WIN_TQ // WINDOW
    pair = 2 * HEAD_DIM
    assert WINDOW == V7X_LANES and nblk >= 4

    rel_ref[...] = (lax.broadcasted_iota(jnp.int32, (span, WINDOW), 0)
                    - lax.broadcasted_iota(jnp.int32, (span, WINDOW), 1)).astype(F32)

    def rows_of(n):
        return pl.ds(n * WINDOW, WINDOW) if isinstance(n, int) else pl.ds(pl.multiple_of(n * WINDOW, WINDOW), WINDOW)

    def window_of(n):
        gn = i * nblk + n
        return jnp.clip(gn - 1, 0, seq // WINDOW - 3), gn

    def prep_q(n):
        qt = q_ref[rows_of(n), :].astype(F32).T
        zeros = jnp.zeros((HEAD_DIM, WINDOW), F32)
        for g in range(WIN_KV_HEADS):
            cols = []
            for hh in range(WIN_GROUP):
                x = qt[(g * WIN_GROUP + hh) * HEAD_DIM:(g * WIN_GROUP + hh + 1) * HEAD_DIM, :]
                cols.append(jnp.concatenate([x, zeros] if g % 2 == 0 else [zeros, x], axis=0))
            w_ref[g] = jnp.concatenate(cols, axis=1).astype(BF16)

    def scores(n, g):
        c0, _ = window_of(n)
        keys = k_ref[pl.ds(pl.multiple_of(c0 * WINDOW, WINDOW), span), (g // 2) * pair:(g // 2 + 1) * pair]
        st_ref[g] = jnp.dot(keys, w_ref[g], preferred_element_type=F32)

    def softmax(n, g):
        c0, gn = window_of(n)
        dist = jnp.abs(((gn - c0) * WINDOW).astype(F32) - rel_ref[...])
        penalty = jnp.where(dist <= WINDOW, 0.0, -NEG)
        for hh in range(WIN_GROUP):
            head = g * WIN_GROUP + hh
            lanes = slice(hh * WINDOW, (hh + 1) * WINDOW)
            s = st_ref[g, :, lanes] - (slopes_ref[head] * dist + penalty)
            sink = sink_ref[head]
            m = jnp.maximum(jnp.max(s, axis=0, keepdims=True), sink)
            p = jnp.exp2(s - m)
            p_ref[g, :, lanes] = p.astype(BF16)
            linv_ref[g, :, lanes] = 1.0 / (jnp.sum(p, axis=0, keepdims=True) + jnp.exp2(sink - m))

    def pv(n, g):
        c0, _ = window_of(n)
        vt = jnp.concatenate([vt_ref[c0 + e, g * HEAD_DIM:(g + 1) * HEAD_DIM, :] for e in range(3)], axis=1)
        ot_ref[g] = jnp.dot(vt, p_ref[g], preferred_element_type=F32) * linv_ref[g]

    def assemble(n):
        for pp in range(WIN_Q_HEADS // 2):
            parts = []
            for head in (2 * pp, 2 * pp + 1):
                g, hh = divmod(head, WIN_GROUP)
                parts.append(ot_ref[g, :, hh * WINDOW:(hh + 1) * WINDOW])
            o_ref[rows_of(n), pp * pair:(pp + 1) * pair] = jnp.concatenate(parts, axis=0).T.astype(BF16)

    def iteration(n, do_assemble, do_pv, do_softmax, do_scores):
        nxt = n + 1 if isinstance(n, int) else jnp.minimum(n + 1, nblk - 1)
        if do_assemble:
            assemble(n - 2)
        if do_scores:
            prep_q(nxt)
        for g in range(WIN_KV_HEADS):
            if do_pv:
                pv(n - 1, g)
            if do_softmax:
                softmax(n, g)
            if do_scores:
                scores(nxt, g)

    prep_q(0)
    for g in range(WIN_KV_HEADS):
        scores(0, g)
    iteration(0, False, False, True, True)
    iteration(1, False, True, True, True)

    def body(n, carry):
        iteration(n, True, True, True, True)
        return carry

    lax.fori_loop(2, nblk, body, 0)
    iteration(nblk, True, True, False, False)
    iteration(nblk + 1, True, False, False, False)


def _win_attn(q, k, vt, sink, batch, seq):
    nq = seq // WIN_TQ
    span = 3 * WINDOW
    slopes = _alibi_slopes(WIN_Q_HEADS) * LOG2E
    est = (2 * 2 * seq * WIN_KV_WIDTH * 2 + 4 * WIN_TQ * TOK_WIDTH * 2
           + WIN_KV_HEADS * span * span * (4 + 2) + 8 * span * span * 4)
    return pl.pallas_call(
        functools.partial(_win_attn_kernel, seq=seq),
        out_shape=jax.ShapeDtypeStruct((batch * seq, TOK_WIDTH), BF16),
        grid=(batch, nq),
        in_specs=[
            pl.BlockSpec(memory_space=pltpu.SMEM),
            pl.BlockSpec(memory_space=pltpu.SMEM),
            pl.BlockSpec((WIN_TQ, TOK_WIDTH), lambda b, i: (b * nq + i, 0)),
            pl.BlockSpec((seq, WIN_KV_WIDTH), lambda b, i: (b, 0)),
            pl.BlockSpec((None, seq // WINDOW, WIN_KV_WIDTH, WINDOW), lambda b, i: (b, 0, 0, 0)),
        ],
        out_specs=pl.BlockSpec((WIN_TQ, TOK_WIDTH), lambda b, i: (b * nq + i, 0)),
        scratch_shapes=[
            pltpu.VMEM((span, WINDOW), F32),
            pltpu.VMEM((WIN_KV_HEADS, 2 * HEAD_DIM, span), BF16),
            pltpu.VMEM((WIN_KV_HEADS, span, span), F32),
            pltpu.VMEM((WIN_KV_HEADS, span, span), BF16),
            pltpu.VMEM((WIN_KV_HEADS, 1, span), F32),
            pltpu.VMEM((WIN_KV_HEADS, HEAD_DIM, span), F32),
        ],
        compiler_params=pltpu.CompilerParams(
            dimension_semantics=("arbitrary", "arbitrary"), vmem_limit_bytes=_vmem_limit(est)),
        name="win_attn",
    )(slopes, sink.astype(F32) * LOG2E, q, k, vt)


def kernel(x, mem, ffn1_w13, ffn1_w2, ln1_g, ln1_b, w_mem_kv, w_out, ln2_g, ln2_b, ffn2_w13, ffn2_w2,
           ln3_g, ln3_b, a_w_in, a_lambda_q1, a_lambda_k1, a_lambda_q2, a_lambda_k2, a_subln_g, b_w_in, b_sink):
    batch, seq, _ = x.shape
    xf = x.reshape(batch * seq, D_MODEL)
    for i in range(DEPTH):
        xf = _ffn_ln(xf, ffn1_w13[i], ffn1_w2[i], ln1_g[i], ln1_b[i])
        j = i // N_MIXERS
        kt, v = _mem_kv(mem, w_mem_kv[i])
        if i % N_MIXERS == 0:
            lam_init = 0.8 - 0.6 * math.exp(-0.3 * i)
            q, k, vt, qm = _in_proj_a(xf, a_w_in[j], batch, seq)
            o = _diff_attn(q, k, vt, a_lambda_q1[j], a_lambda_k1[j], a_lambda_q2[j], a_lambda_k2[j],
                           a_subln_g[j], lam_init, batch, seq)
            xf = _mix_out(xf, o, qm, 0, kt, v, w_out[i], ln2_g[i], ln2_b[i], seq)
        else:
            q, k, vt, qm = _in_proj_b(xf, b_w_in[j], batch, seq)
            o = _win_attn(q, k, vt, b_sink[j], batch, seq)
            xf = _mix_out(xf, o, qm, 0, kt, v, w_out[i], ln2_g[i], ln2_b[i], seq)
        xf = _ffn_ln(xf, ffn2_w13[i], ffn2_w2[i], ln3_g[i], ln3_b[i])
    return xf.reshape(batch, seq, D_MODEL)
```

```python
import functools
import math

import numpy as np
import jax
import jax.numpy as jnp
from jax import lax
from jax.experimental import pallas as pl
from jax.experimental.pallas import tpu as pltpu

D_MODEL = 1024
DEPTH = 2
N_MIXERS = 2
N_MEM = 256
HEAD_DIM = 64
MEM_HEADS = 4
MEM_WIDTH = MEM_HEADS * HEAD_DIM
TOK_WIDTH = D_MODEL - MEM_WIDTH
DIFF_HEADS = TOK_WIDTH // (2 * HEAD_DIM)
DIFF_V_DIM = 2 * HEAD_DIM
QK_A = DIFF_HEADS * 2 * HEAD_DIM
WIN_Q_HEADS = TOK_WIDTH // HEAD_DIM
WIN_GROUP = 3
WIN_KV_HEADS = WIN_Q_HEADS // WIN_GROUP
WIN_KV_WIDTH = WIN_KV_HEADS * HEAD_DIM
WINDOW = 128
D_FF = 2816
A_IN = 2 * QK_A + TOK_WIDTH + MEM_WIDTH
B_IN = TOK_WIDTH + 2 * WIN_KV_WIDTH + MEM_WIDTH
N_A = (DEPTH + 1) // 2
DN_ALPHA = (2 * DEPTH) ** 0.25
LN_EPS = 1e-5
NEG = -1e30
QK_SCALE = HEAD_DIM ** -0.5
LOG2E = math.log2(math.e)

F32 = jnp.float32
BF16 = jnp.bfloat16

V7X_LANES = 128
V7X_MXU_DIM = 256
V7X_VMEM_BYTES = 64 * 1024 * 1024

FFN_TM = 512
FF_CHUNK = V7X_MXU_DIM
PROJ_TM = 1024
PROJ_CHUNK = V7X_MXU_DIM
DIFF_TQ = 256
DIFF_TK = 256
DIFF_DEPTH = 4
MIX_TM = 512
WIN_TQ = 1024


def _vmem_limit(nbytes):
    return int(min(V7X_VMEM_BYTES - 8 * 1024 * 1024, nbytes * 3 // 2))


def _resident(block_shape, index_map):
    return pl.BlockSpec(block_shape, index_map, pipeline_mode=pl.Buffered(1))


def _layer_norm(y, g, b):
    mu = jnp.mean(y, axis=-1, keepdims=True)
    yc = y - mu
    var = jnp.mean(yc * yc, axis=-1, keepdims=True)
    return yc * lax.rsqrt(var + LN_EPS) * g + b


def _alibi_slopes(n):
    return jnp.asarray(2.0 ** (-8.0 * np.arange(1, n + 1) / n), dtype=F32)


def _ffn_ln_kernel(x_ref, wg_ref, wu_ref, w2_ref, g_ref, b_ref, o_ref, acc_ref):
    x = x_ref[...]
    xb = x.astype(BF16)
    for c in range(D_FF // FF_CHUNK):
        cols = slice(c * FF_CHUNK, (c + 1) * FF_CHUNK)
        gate = jnp.dot(xb, wg_ref[:, cols], preferred_element_type=F32)
        up = jnp.dot(xb, wu_ref[:, cols], preferred_element_type=F32)
        act = (gate / (1.0 + jnp.exp(-gate)) * up).astype(BF16)
        part = jnp.dot(act, w2_ref[cols, :], preferred_element_type=F32)
        if c == 0:
            acc_ref[...] = part
        else:
            acc_ref[...] += part
    y = DN_ALPHA * x + 0.5 * acc_ref[...]
    o_ref[...] = _layer_norm(y, g_ref[...], b_ref[...])


def _ffn_ln(x, w13, w2, layer, g, b):
    n = x.shape[0]
    est = (2 * D_MODEL * D_FF + D_FF * D_MODEL) * 2 + 5 * FFN_TM * D_MODEL * 4 + 4 * FFN_TM * FF_CHUNK * 4
    return pl.pallas_call(
        _ffn_ln_kernel,
        out_shape=jax.ShapeDtypeStruct((n, D_MODEL), F32),
        grid=(n // FFN_TM,),
        in_specs=[
            pl.BlockSpec((FFN_TM, D_MODEL), lambda i: (i, 0)),
            _resident((None, D_MODEL, D_FF), lambda i: (layer, 0, 0)),
            _resident((None, D_MODEL, D_FF), lambda i: (layer, 0, 1)),
            _resident((None, D_FF, D_MODEL), lambda i: (layer, 0, 0)),
            _resident((1, D_MODEL), lambda i: (0, 0)),
            _resident((1, D_MODEL), lambda i: (0, 0)),
        ],
        out_specs=pl.BlockSpec((FFN_TM, D_MODEL), lambda i: (i, 0)),
        scratch_shapes=[pltpu.VMEM((FFN_TM, D_MODEL), F32)],
        compiler_params=pltpu.CompilerParams(
            dimension_semantics=("arbitrary",), vmem_limit_bytes=_vmem_limit(est)),
        name="ffn_ln",
    )(x, w13, w13, w2, g.reshape(1, D_MODEL), b.reshape(1, D_MODEL))


def _in_proj_a_kernel(x_ref, w_ref, q_ref, k_ref, vt_ref, qm_ref):
    xb = x_ref[...].astype(BF16)
    heads_per_chunk = PROJ_CHUNK // DIFF_V_DIM
    n_q = QK_A // PROJ_CHUNK
    n_qk = 2 * n_q
    n_v = TOK_WIDTH // PROJ_CHUNK
    for c in range(A_IN // PROJ_CHUNK):
        r = jnp.dot(xb, w_ref[:, c * PROJ_CHUNK:(c + 1) * PROJ_CHUNK], preferred_element_type=F32)
        if c < n_qk:
            dst, c0, scale = (q_ref, 0, QK_SCALE * LOG2E) if c < n_q else (k_ref, n_q, 1.0)
            for hh in range(heads_per_chunk):
                dst[(c - c0) * heads_per_chunk + hh] = (
                    r[:, hh * DIFF_V_DIM:(hh + 1) * DIFF_V_DIM] * scale).astype(BF16)
        elif c < n_qk + n_v:
            rt = r.T.astype(BF16)
            for hh in range(heads_per_chunk):
                for t in range(PROJ_TM // DIFF_TK):
                    vt_ref[(c - n_qk) * heads_per_chunk + hh, t] = rt[
                        hh * DIFF_V_DIM:(hh + 1) * DIFF_V_DIM, t * DIFF_TK:(t + 1) * DIFF_TK]
        else:
            qm_ref[...] = (r * QK_SCALE).astype(BF16)


def _in_proj_a(x, w_in, batch, seq):
    n = x.shape[0]
    w_in = w_in.astype(BF16)
    tiles_per_batch = seq // PROJ_TM
    chunks_per_tile = PROJ_TM // DIFF_TK
    est = D_MODEL * A_IN * 2 + 2 * PROJ_TM * D_MODEL * 4 + 2 * PROJ_TM * A_IN * 2 + 4 * PROJ_TM * PROJ_CHUNK * 4
    head_major = jax.ShapeDtypeStruct((batch, DIFF_HEADS, seq, DIFF_V_DIM), BF16)
    head_major_spec = pl.BlockSpec((None, DIFF_HEADS, PROJ_TM, DIFF_V_DIM),
                                   lambda i: (i // tiles_per_batch, 0, i % tiles_per_batch, 0))
    return pl.pallas_call(
        _in_proj_a_kernel,
        out_shape=(
            head_major,
            head_major,
            jax.ShapeDtypeStruct((batch, DIFF_HEADS, seq // DIFF_TK, DIFF_V_DIM, DIFF_TK), BF16),
            jax.ShapeDtypeStruct((n, MEM_WIDTH), BF16),
        ),
        grid=(n // PROJ_TM,),
        in_specs=[
            pl.BlockSpec((PROJ_TM, D_MODEL), lambda i: (i, 0)),
            _resident((D_MODEL, A_IN), lambda i: (0, 0)),
        ],
        out_specs=(
            head_major_spec,
            head_major_spec,
            pl.BlockSpec((None, DIFF_HEADS, chunks_per_tile, DIFF_V_DIM, DIFF_TK),
                         lambda i: (i // tiles_per_batch, 0, i % tiles_per_batch, 0, 0)),
            pl.BlockSpec((PROJ_TM, MEM_WIDTH), lambda i: (i, 0)),
        ),
        compiler_params=pltpu.CompilerParams(
            dimension_semantics=("arbitrary",), vmem_limit_bytes=_vmem_limit(est)),
        name="in_proj_a",
    )(x, w_in)


def _diff_attn_kernel(slopes_ref, q_ref, k_ref, vt_ref, lq1_ref, lk1_ref, lq2_ref, lk2_ref, g_ref,
                      o_ref, qt_ref, pos_ref, bias_ref, sr_ref, m_ref, l_ref, acc_ref,
                      st_ref, shift_ref, max_ref, p_ref, alpha_ref, *, lam_init, seq):
    h = pl.program_id(1)
    i = pl.program_id(2)
    tq, tk = DIFF_TQ, DIFF_TK
    assert tq == tk and tk <= 256
    n_steps = seq // tk
    n_strips = 2 * tq // V7X_LANES
    slope = slopes_ref[h]

    qt = q_ref[...].astype(F32).T
    row = lax.broadcasted_iota(jnp.int32, qt.shape, 0)
    zero = jnp.zeros_like(qt)
    qt_ref[0:DIFF_V_DIM, :tq] = jnp.where(row < HEAD_DIM, qt, zero).astype(BF16)
    qt_ref[0:DIFF_V_DIM, tq:] = jnp.where(row >= HEAD_DIM, qt, zero).astype(BF16)
    slope_v = jnp.full((16, 2 * tq), slope, F32)
    s_hi = slope_v.astype(BF16).astype(F32)
    s_mid = (slope_v - s_hi).astype(BF16).astype(F32)
    s_lo = slope_v - s_hi - s_mid
    prow = lax.broadcasted_iota(jnp.int32, (16, 2 * tq), 0)
    pieces = jnp.where(prow == 0, s_hi, jnp.where(prow == 1, s_mid, jnp.where(prow == 2, s_lo, 0.0)))
    qt_ref[DIFF_V_DIM:DIFF_V_DIM + 16, :] = pieces.astype(BF16)
    qt_ref[DIFF_V_DIM + 16:, :] = jnp.zeros((DIFF_V_DIM - 16, 2 * tq), BF16)

    r_s = lax.broadcasted_iota(jnp.int32, (tk, V7X_LANES), 0).astype(F32)
    lane = lax.broadcasted_iota(jnp.int32, (tk, V7X_LANES), 1)
    r_s = jnp.where(lane < 3, r_s, 0.0)
    pos_ref[0] = r_s.astype(BF16)
    pos_ref[1] = (-r_s).astype(BF16)
    pos_ref[2] = jnp.zeros((tk, V7X_LANES), BF16)

    rel = (lax.broadcasted_iota(jnp.int32, (tk, tq), 0) - lax.broadcasted_iota(jnp.int32, (tk, tq), 1))
    bias_ref[...] = slope * jnp.abs(rel).astype(F32)
    r_t = lax.broadcasted_iota(jnp.int32, (1, 2 * tq), 1)
    sr_ref[...] = slope * jnp.where(r_t >= tq, r_t - tq, r_t).astype(F32)

    m_ref[...] = jnp.full(m_ref.shape, -jnp.inf, F32)
    l_ref[...] = jnp.zeros(l_ref.shape, F32)
    acc_ref[...] = jnp.zeros(acc_ref.shape, F32)

    def keys(j):
        return k_ref[pl.ds(pl.multiple_of(j * tk, tk), tk), :]

    def other_tile(u):
        return jnp.where(u >= i, u + 1, u)

    def slot_tiles(s):
        first = i if (isinstance(s, int) and s == 0) else jnp.where(s == 0, i, other_tile(2 * s - 1))
        return first, other_tile(2 * s)

    def strip(c):
        return slice(c * V7X_LANES, (c + 1) * V7X_LANES)

    def scores(s, b):
        halves = []
        for e, j in enumerate(slot_tiles(s)):
            if isinstance(s, int) and s == 0 and e == 0:
                halves.append(jnp.concatenate([keys(j), pos_ref[2]], axis=1))
                shift_ref[b, e] = jnp.zeros((1, 2 * tq), F32)
            else:
                after = (j > i).astype(jnp.int32)
                halves.append(jnp.concatenate([keys(j), pos_ref[after]], axis=1))
                sign = (1 - 2 * after).astype(F32)
                tile_dist = (jnp.abs(j - i) * tk).astype(F32)
                shift_ref[b, e] = -(sign * sr_ref[...]) - slope * tile_dist
        st = jnp.dot(jnp.concatenate(halves, axis=0), qt_ref[...], preferred_element_type=F32)
        for e in range(2):
            for c in range(n_strips):
                blk = st[e * tk:(e + 1) * tk, strip(c)]
                if isinstance(s, int) and s == 0 and e == 0:
                    blk = blk - bias_ref[:, strip(c % (tq // V7X_LANES))]
                st_ref[b, e, c] = blk
                max_ref[b, e, :, strip(c)] = jnp.max(blk, axis=0, keepdims=True)

    def softmax(b):
        for c in range(n_strips):
            lanes = strip(c)
            shift0, shift1 = shift_ref[b, 0, :, lanes], shift_ref[b, 1, :, lanes]
            m_old = m_ref[:, lanes]
            m_new = jnp.maximum(m_old, jnp.maximum(max_ref[b, 0, :, lanes] + shift0,
                                                   max_ref[b, 1, :, lanes] + shift1))
            alpha = jnp.exp2(m_old - m_new)
            p0 = jnp.exp2(st_ref[b, 0, c] - (m_new - shift0))
            p1 = jnp.exp2(st_ref[b, 1, c] - (m_new - shift1))
            l_ref[:, lanes] = alpha * l_ref[:, lanes] + (jnp.sum(p0, axis=0, keepdims=True)
                                                         + jnp.sum(p1, axis=0, keepdims=True))
            m_ref[:, lanes] = m_new
            p_ref[b, 0, c] = p0.astype(BF16)
            p_ref[b, 1, c] = p1.astype(BF16)
            alpha_ref[b, :, lanes] = alpha

    def pv(s, b):
        j0, j1 = slot_tiles(s)
        p = jnp.concatenate([jnp.concatenate([p_ref[b, e, c] for c in range(n_strips)], axis=1)
                             for e in range(2)], axis=0)
        v = jnp.concatenate([vt_ref[j0], vt_ref[j1]], axis=1)
        acc_ref[...] = alpha_ref[b] * acc_ref[...] + jnp.dot(v, p, preferred_element_type=F32)

    depth = DIFF_DEPTH
    n_slots = n_steps // 2
    assert n_steps % 2 == 0 and (n_slots - 2 * depth) % depth == 0

    def stage(s, r):
        if not (isinstance(s, int) and s < depth):
            pv(s - depth, r)
        softmax(r)
        if not (isinstance(s, int) and s + depth >= n_slots):
            scores(s + depth, r)

    for s in range(depth):
        scores(s, s)
    for s in range(depth):
        stage(s, s)

    def body(k, carry):
        for r in range(depth):
            stage(depth * (k + 1) + r, r)
        return carry

    lax.fori_loop(0, (n_slots - 2 * depth) // depth, body, 0)
    for s in range(n_slots - depth, n_slots):
        stage(s, s % depth)
    for s in range(n_slots - depth, n_slots):
        pv(s, s % depth)

    lam = (jnp.exp(jnp.sum(lq1_ref[...] * lk1_ref[...], axis=-1, keepdims=True))
           - jnp.exp(jnp.sum(lq2_ref[...] * lk2_ref[...], axis=-1, keepdims=True)) + lam_init)
    o_t = acc_ref[...] / l_ref[...]
    a = (o_t[:, :tq] - lam * o_t[:, tq:]).T
    a = a * lax.rsqrt(jnp.mean(a * a, axis=-1, keepdims=True) + LN_EPS) * g_ref[...]
    o_ref[...] = (a * (1.0 - lam_init)).astype(BF16)


def _diff_attn(q, k, vt, lq1, lk1, lq2, lk2, subln_g, lam_init, batch, seq):
    nq = seq // DIFF_TQ
    strips = (2 * DIFF_TQ // V7X_LANES, DIFF_TK, V7X_LANES)
    slopes = _alibi_slopes(DIFF_HEADS) * LOG2E
    vec = lambda v: v.reshape(1, -1).astype(F32)
    est = (2 * 2 * seq * DIFF_V_DIM * 2 + 4 * DIFF_TQ * DIFF_V_DIM * 2
           + DIFF_DEPTH * 2 * DIFF_TK * 2 * DIFF_TQ * (4 + 2)
           + 4 * DIFF_TK * 2 * DIFF_TQ * 4 + 2 * DIFF_V_DIM * 2 * DIFF_TQ * 4)
    small = lambda n: pl.BlockSpec((1, n), lambda b, h, i: (0, 0))
    return pl.pallas_call(
        functools.partial(_diff_attn_kernel, lam_init=lam_init, seq=seq),
        out_shape=jax.ShapeDtypeStruct((batch * seq, TOK_WIDTH), BF16),
        grid=(batch, DIFF_HEADS, nq),
        in_specs=[
            pl.BlockSpec(memory_space=pltpu.SMEM),
            pl.BlockSpec((None, None, DIFF_TQ, DIFF_V_DIM), lambda b, h, i: (b, h, i, 0)),
            pl.BlockSpec((None, None, seq, DIFF_V_DIM), lambda b, h, i: (b, h, 0, 0)),
            pl.BlockSpec((None, None, seq // DIFF_TK, DIFF_V_DIM, DIFF_TK), lambda b, h, i: (b, h, 0, 0, 0)),
            small(HEAD_DIM), small(HEAD_DIM), small(HEAD_DIM), small(HEAD_DIM), small(DIFF_V_DIM),
        ],
        out_specs=pl.BlockSpec((DIFF_TQ, DIFF_V_DIM), lambda b, h, i: (b * nq + i, h)),
        scratch_shapes=[
            pltpu.VMEM((2 * DIFF_V_DIM, 2 * DIFF_TQ), BF16),
            pltpu.VMEM((3, DIFF_TK, V7X_LANES), BF16),
            pltpu.VMEM((DIFF_TK, DIFF_TQ), F32),
            pltpu.VMEM((1, 2 * DIFF_TQ), F32),
            pltpu.VMEM((1, 2 * DIFF_TQ), F32),
            pltpu.VMEM((1, 2 * DIFF_TQ), F32),
            pltpu.VMEM((DIFF_V_DIM, 2 * DIFF_TQ), F32),
            pltpu.VMEM((DIFF_DEPTH, 2) + strips, F32),
            pltpu.VMEM((DIFF_DEPTH, 2, 1, 2 * DIFF_TQ), F32),
            pltpu.VMEM((DIFF_DEPTH, 2, 1, 2 * DIFF_TQ), F32),
            pltpu.VMEM((DIFF_DEPTH, 2) + strips, BF16),
            pltpu.VMEM((DIFF_DEPTH, 1, 2 * DIFF_TQ), F32),
        ],
        compiler_params=pltpu.CompilerParams(
            dimension_semantics=("arbitrary", "arbitrary", "arbitrary"), vmem_limit_bytes=_vmem_limit(est)),
        name="diff_attn",
    )(slopes, q, k, vt, vec(lq1), vec(lk1), vec(lq2), vec(lk2), vec(subln_g))


def _mem_kv_kernel(mem_ref, w_ref, kt_ref, v_ref):
    kv = jnp.dot(mem_ref[...].astype(BF16), w_ref[...], preferred_element_type=F32)
    kt = kv[:, :MEM_WIDTH].T
    v = kv[:, MEM_WIDTH:]
    pair_rows = 2 * HEAD_DIM
    row = lax.broadcasted_iota(jnp.int32, (pair_rows, N_MEM), 0)
    col = lax.broadcasted_iota(jnp.int32, (N_MEM, MEM_WIDTH), 1)
    for h in range(MEM_HEADS):
        pair = kt[(h // 2) * pair_rows:(h // 2 + 1) * pair_rows, :]
        keep = (row >= (h % 2) * HEAD_DIM) & (row < (h % 2 + 1) * HEAD_DIM)
        kt_ref[h] = jnp.where(keep, pair, jnp.zeros_like(pair)).astype(BF16)
        keep_v = (col >= h * HEAD_DIM) & (col < (h + 1) * HEAD_DIM)
        v_ref[h] = jnp.where(keep_v, v, jnp.zeros_like(v)).astype(BF16)


def _mem_kv(mem, w_mem_kv):
    batch = mem.shape[0]
    return pl.pallas_call(
        _mem_kv_kernel,
        out_shape=(
            jax.ShapeDtypeStruct((batch, MEM_HEADS, 2 * HEAD_DIM, N_MEM), BF16),
            jax.ShapeDtypeStruct((batch, MEM_HEADS, N_MEM, MEM_WIDTH), BF16),
        ),
        grid=(batch,),
        in_specs=[
            pl.BlockSpec((None, N_MEM, D_MODEL), lambda b: (b, 0, 0)),
            pl.BlockSpec((D_MODEL, 2 * MEM_WIDTH), lambda b: (0, 0)),
        ],
        out_specs=(
            pl.BlockSpec((None, MEM_HEADS, 2 * HEAD_DIM, N_MEM), lambda b: (b, 0, 0, 0)),
            pl.BlockSpec((None, MEM_HEADS, N_MEM, MEM_WIDTH), lambda b: (b, 0, 0, 0)),
        ),
        name="mem_kv",
    )(mem, w_mem_kv.astype(BF16))


def _mix_out_kernel(x_ref, o_ref, qm_ref, kt_ref, v_ref, wo_ref, wm_ref, g_ref, b_ref, out_ref):
    qm = qm_ref[...]
    pair_w = 2 * HEAD_DIM
    mem_o = None
    for h in range(MEM_HEADS):
        q_pair = qm[:, (h // 2) * pair_w:(h // 2 + 1) * pair_w]
        s = jnp.dot(q_pair, kt_ref[h], preferred_element_type=F32)
        e = jnp.exp(s - jnp.max(s, axis=-1, keepdims=True))
        p = (e / jnp.sum(e, axis=-1, keepdims=True)).astype(BF16)
        part = jnp.dot(p, v_ref[h], preferred_element_type=F32)
        mem_o = part if mem_o is None else mem_o + part
    y = jnp.dot(o_ref[...], wo_ref[...], preferred_element_type=F32)
    y = y + jnp.dot(mem_o.astype(BF16), wm_ref[...], preferred_element_type=F32)
    out_ref[...] = _layer_norm(DN_ALPHA * x_ref[...] + y, g_ref[...], b_ref[...])


def _mix_out(x, o, qm, kt, v, w_out, layer, g, b, seq):
    n = x.shape[0]
    tiles_per_batch = seq // MIX_TM
    est = (D_MODEL * D_MODEL * 2 + 4 * MIX_TM * D_MODEL * 4 + 2 * MIX_TM * D_MODEL * 2
           + 4 * MEM_HEADS * N_MEM * MEM_WIDTH * 2 + 6 * MIX_TM * D_MODEL * 4)
    return pl.pallas_call(
        _mix_out_kernel,
        out_shape=jax.ShapeDtypeStruct((n, D_MODEL), F32),
        grid=(n // MIX_TM,),
        in_specs=[
            pl.BlockSpec((MIX_TM, D_MODEL), lambda i: (i, 0)),
            pl.BlockSpec((MIX_TM, TOK_WIDTH), lambda i: (i, 0)),
            pl.BlockSpec((MIX_TM, MEM_WIDTH), lambda i: (i, 0)),
            pl.BlockSpec((None, MEM_HEADS, 2 * HEAD_DIM, N_MEM), lambda i: (i // tiles_per_batch, 0, 0, 0)),
            pl.BlockSpec((None, MEM_HEADS, N_MEM, MEM_WIDTH), lambda i: (i // tiles_per_batch, 0, 0, 0)),
            _resident((None, TOK_WIDTH, D_MODEL), lambda i: (layer, 0, 0)),
            _resident((None, MEM_WIDTH, D_MODEL), lambda i: (layer, TOK_WIDTH // MEM_WIDTH, 0)),
            _resident((1, D_MODEL), lambda i: (0, 0)),
            _resident((1, D_MODEL), lambda i: (0, 0)),
        ],
        out_specs=pl.BlockSpec((MIX_TM, D_MODEL), lambda i: (i, 0)),
        compiler_params=pltpu.CompilerParams(
            dimension_semantics=("arbitrary",), vmem_limit_bytes=_vmem_limit(est)),
        name="mix_out",
    )(x, o, qm, kt, v, w_out, w_out, g.reshape(1, D_MODEL), b.reshape(1, D_MODEL))


def _in_proj_b_kernel(x_ref, w_ref, q_ref, k_ref, vt_ref, qm_ref):
    assert WIN_KV_WIDTH == PROJ_CHUNK and MEM_WIDTH == PROJ_CHUNK
    xb = x_ref[...].astype(BF16)
    n_q = TOK_WIDTH // PROJ_CHUNK
    for c in range(B_IN // PROJ_CHUNK):
        cols = slice(c * PROJ_CHUNK, (c + 1) * PROJ_CHUNK)
        r = jnp.dot(xb, w_ref[:, cols], preferred_element_type=F32)
        if c < n_q:
            q_ref[:, cols] = (r * (QK_SCALE * LOG2E)).astype(BF16)
        elif c == n_q:
            k_ref[...] = r.astype(BF16)
        elif c == n_q + 1:
            rt = r.T.astype(BF16)
            for t in range(PROJ_TM // WINDOW):
                vt_ref[t] = rt[:, t * WINDOW:(t + 1) * WINDOW]
        else:
            qm_ref[...] = (r * QK_SCALE).astype(BF16)


def _in_proj_b(x, w_in, batch, seq):
    n = x.shape[0]
    tiles_per_batch = seq // PROJ_TM
    est = D_MODEL * B_IN * 2 + 2 * PROJ_TM * D_MODEL * 4 + 2 * PROJ_TM * B_IN * 2 + 4 * PROJ_TM * PROJ_CHUNK * 4
    return pl.pallas_call(
        _in_proj_b_kernel,
        out_shape=(
            jax.ShapeDtypeStruct((n, TOK_WIDTH), BF16),
            jax.ShapeDtypeStruct((n, WIN_KV_WIDTH), BF16),
            jax.ShapeDtypeStruct((batch, seq // WINDOW, WIN_KV_WIDTH, WINDOW), BF16),
            jax.ShapeDtypeStruct((n, MEM_WIDTH), BF16),
        ),
        grid=(n // PROJ_TM,),
        in_specs=[
            pl.BlockSpec((PROJ_TM, D_MODEL), lambda i: (i, 0)),
            _resident((D_MODEL, B_IN), lambda i: (0, 0)),
        ],
        out_specs=(
            pl.BlockSpec((PROJ_TM, TOK_WIDTH), lambda i: (i, 0)),
            pl.BlockSpec((PROJ_TM, WIN_KV_WIDTH), lambda i: (i, 0)),
            pl.BlockSpec((None, PROJ_TM // WINDOW, WIN_KV_WIDTH, WINDOW),
                         lambda i: (i // tiles_per_batch, i % tiles_per_batch, 0, 0)),
            pl.BlockSpec((PROJ_TM, MEM_WIDTH), lambda i: (i, 0)),
        ),
        compiler_params=pltpu.CompilerParams(
            dimension_semantics=("arbitrary",), vmem_limit_bytes=_vmem_limit(est)),
        name="in_proj_b",
    )(x, w_in.astype(BF16))


def _win_attn_kernel(slopes_ref, sink_ref, q_ref, k_ref, vt_ref, o_ref,
                     rel_ref, w_ref, st_ref, p_ref, linv_ref, ot_ref, *, seq):
    i = pl.program_id(1)
    span = 3 * WINDOW
    nblk = WIN_TQ // WINDOW
    pair = 2 * HEAD_DIM
    assert WINDOW == V7X_LANES and nblk >= 4

    rel_ref[...] = (lax.broadcasted_iota(jnp.int32, (span, WINDOW), 0)
                    - lax.broadcasted_iota(jnp.int32, (span, WINDOW), 1)).astype(F32)

    def rows_of(n):
        return pl.ds(n * WINDOW, WINDOW) if isinstance(n, int) else pl.ds(pl.multiple_of(n * WINDOW, WINDOW), WINDOW)

    def window_of(n):
        gn = i * nblk + n
        return jnp.clip(gn - 1, 0, seq // WINDOW - 3), gn

    def prep_q(n):
        qt = q_ref[rows_of(n), :].astype(F32).T
        zeros = jnp.zeros((HEAD_DIM, WINDOW), F32)
        for g in range(WIN_KV_HEADS):
            cols = []
            for hh in range(WIN_GROUP):
                x = qt[(g * WIN_GROUP + hh) * HEAD_DIM:(g * WIN_GROUP + hh + 1) * HEAD_DIM, :]
                cols.append(jnp.concatenate([x, zeros] if g % 2 == 0 else [zeros, x], axis=0))
            w_ref[g] = jnp.concatenate(cols, axis=1).astype(BF16)

    def scores(n, g):
        c0, _ = window_of(n)
        keys = k_ref[pl.ds(pl.multiple_of(c0 * WINDOW, WINDOW), span), (g // 2) * pair:(g // 2 + 1) * pair]
        st_ref[g] = jnp.dot(keys, w_ref[g], preferred_element_type=F32)

    def softmax(n, g):
        c0, gn = window_of(n)
        dist = jnp.abs(((gn - c0) * WINDOW).astype(F32) - rel_ref[...])
        penalty = jnp.where(dist <= WINDOW, 0.0, -NEG)
        for hh in range(WIN_GROUP):
            head = g * WIN_GROUP + hh
            lanes = slice(hh * WINDOW, (hh + 1) * WINDOW)
            s = st_ref[g, :, lanes] - (slopes_ref[head] * dist + penalty)
            sink = sink_ref[head]
            m = jnp.maximum(jnp.max(s, axis=0, keepdims=True), sink)
            p = jnp.exp2(s - m)
            p_ref[g, :, lanes] = p.astype(BF16)
            linv_ref[g, :, lanes] = 1.0 / (jnp.sum(p, axis=0, keepdims=True) + jnp.exp2(sink - m))

    def pv(n, g):
        c0, _ = window_of(n)
        vt = jnp.concatenate([vt_ref[c0 + e, g * HEAD_DIM:(g + 1) * HEAD_DIM, :] for e in range(3)], axis=1)
        ot_ref[g] = jnp.dot(vt, p_ref[g], preferred_element_type=F32) * linv_ref[g]

    def assemble(n):
        for pp in range(WIN_Q_HEADS // 2):
            parts = []
            for head in (2 * pp, 2 * pp + 1):
                g, hh = divmod(head, WIN_GROUP)
                parts.append(ot_ref[g, :, hh * WINDOW:(hh + 1) * WINDOW])
            o_ref[rows_of(n), pp * pair:(pp + 1) * pair] = jnp.concatenate(parts, axis=0).T.astype(BF16)

    def iteration(n, do_assemble, do_pv, do_softmax, do_scores):
        nxt = min(n + 1, nblk - 1) if isinstance(n, int) else jnp.minimum(n + 1, nblk - 1)
        if do_assemble:
            assemble(n - 2)
        if do_scores:
            prep_q(nxt)
        for g in range(WIN_KV_HEADS):
            if do_pv:
                pv(n - 1, g)
            if do_softmax:
                softmax(n, g)
            if do_scores:
                scores(nxt, g)

    prep_q(0)
    for g in range(WIN_KV_HEADS):
        scores(0, g)
    iteration(0, False, False, True, True)
    iteration(1, False, True, True, True)

    def body(n, carry):
        iteration(n, True, True, True, True)
        return carry

    lax.fori_loop(2, nblk, body, 0)
    iteration(nblk, True, True, False, False)
    iteration(nblk + 1, True, False, False, False)


def _win_attn(q, k, vt, sink, batch, seq):
    nq = seq // WIN_TQ
    span = 3 * WINDOW
    slopes = _alibi_slopes(WIN_Q_HEADS) * LOG2E
    est = (2 * 2 * seq * WIN_KV_WIDTH * 2 + 4 * WIN_TQ * TOK_WIDTH * 2
           + WIN_KV_HEADS * span * span * (4 + 2) + 8 * span * span * 4)
    return pl.pallas_call(
        functools.partial(_win_attn_kernel, seq=seq),
        out_shape=jax.ShapeDtypeStruct((batch * seq, TOK_WIDTH), BF16),
        grid=(batch, nq),
        in_specs=[
            pl.BlockSpec(memory_space=pltpu.SMEM),
            pl.BlockSpec(memory_space=pltpu.SMEM),
            pl.BlockSpec((WIN_TQ, TOK_WIDTH), lambda b, i: (b * nq + i, 0)),
            pl.BlockSpec((seq, WIN_KV_WIDTH), lambda b, i: (b, 0)),
            pl.BlockSpec((None, seq // WINDOW, WIN_KV_WIDTH, WINDOW), lambda b, i: (b, 0, 0, 0)),
        ],
        out_specs=pl.BlockSpec((WIN_TQ, TOK_WIDTH), lambda b, i: (b * nq + i, 0)),
        scratch_shapes=[
            pltpu.VMEM((span, WINDOW), F32),
            pltpu.VMEM((WIN_KV_HEADS, 2 * HEAD_DIM, span), BF16),
            pltpu.VMEM((WIN_KV_HEADS, span, span), F32),
            pltpu.VMEM((WIN_KV_HEADS, span, span), BF16),
            pltpu.VMEM((WIN_KV_HEADS, 1, span), F32),
            pltpu.VMEM((WIN_KV_HEADS, HEAD_DIM, span), F32),
        ],
        compiler_params=pltpu.CompilerParams(
            dimension_semantics=("arbitrary", "arbitrary"), vmem_limit_bytes=_vmem_limit(est)),
        name="win_attn",
    )(slopes, sink.astype(F32) * LOG2E, q, k, vt)


def kernel(x, mem, ffn1_w13, ffn1_w2, ln1_g, ln1_b, w_mem_kv, w_out, ln2_g, ln2_b, ffn2_w13, ffn2_w2,
           ln3_g, ln3_b, a_w_in, a_lambda_q1, a_lambda_k1, a_lambda_q2, a_lambda_k2, a_subln_g, b_w_in, b_sink):
    batch, seq, _ = x.shape
    xf = x.reshape(batch * seq, D_MODEL)
    ffn1_w13, ffn1_w2, ffn2_w13, ffn2_w2, w_out = (
        w.astype(BF16) for w in (ffn1_w13, ffn1_w2, ffn2_w13, ffn2_w2, w_out))
    for i in range(DEPTH):
        xf = _ffn_ln(xf, ffn1_w13, ffn1_w2, i, ln1_g[i], ln1_b[i])
        j = i // N_MIXERS
        kt, v = _mem_kv(mem, w_mem_kv[i])
        if i % N_MIXERS == 0:
            lam_init = 0.8 - 0.6 * math.exp(-0.3 * i)
            q, k, vt, qm = _in_proj_a(xf, a_w_in[j], batch, seq)
            o = _diff_attn(q, k, vt, a_lambda_q1[j], a_lambda_k1[j], a_lambda_q2[j], a_lambda_k2[j],
                           a_subln_g[j], lam_init, batch, seq)
        else:
            q, k, vt, qm = _in_proj_b(xf, b_w_in[j], batch, seq)
            o = _win_attn(q, k, vt, b_sink[j], batch, seq)
        xf = _mix_out(xf, o, qm, kt, v, w_out, i, ln2_g[i], ln2_b[i], seq)
        xf = _ffn_ln(xf, ffn2_w13, ffn2_w2, i, ln3_g[i], ln3_b[i])
    return xf.reshape(batch, seq, D_MODEL)
```

```python
import functools
import math

import numpy as np
import jax
import jax.numpy as jnp
from jax import lax
from jax.experimental import pallas as pl
from jax.experimental.pallas import tpu as pltpu

D_MODEL = 1024
DEPTH = 2
N_MIXERS = 2
N_MEM = 256
HEAD_DIM = 64
MEM_HEADS = 4
MEM_WIDTH = MEM_HEADS * HEAD_DIM
TOK_WIDTH = D_MODEL - MEM_WIDTH
DIFF_HEADS = TOK_WIDTH // (2 * HEAD_DIM)
DIFF_V_DIM = 2 * HEAD_DIM
QK_A = DIFF_HEADS * 2 * HEAD_DIM
WIN_Q_HEADS = TOK_WIDTH // HEAD_DIM
WIN_GROUP = 3
WIN_KV_HEADS = WIN_Q_HEADS // WIN_GROUP
WIN_KV_WIDTH = WIN_KV_HEADS * HEAD_DIM
WINDOW = 128
D_FF = 2816
A_IN = 2 * QK_A + TOK_WIDTH + MEM_WIDTH
B_IN = TOK_WIDTH + 2 * WIN_KV_WIDTH + MEM_WIDTH
N_A = (DEPTH + 1) // 2
DN_ALPHA = (2 * DEPTH) ** 0.25
LN_EPS = 1e-5
NEG = -1e30
QK_SCALE = HEAD_DIM ** -0.5
LOG2E = math.log2(math.e)

F32 = jnp.float32
BF16 = jnp.bfloat16

V7X_LANES = 128
V7X_MXU_DIM = 256
V7X_VMEM_BYTES = 64 * 1024 * 1024

FFN_TM = 512
FF_CHUNK = V7X_MXU_DIM
PROJ_TM = 1024
PROJ_CHUNK = V7X_MXU_DIM
DIFF_TQ = 256
DIFF_TK = 256
DIFF_DEPTH = 4
MIX_TM = 512
WIN_TQ = 1024


def _vmem_limit(nbytes):
    return int(min(V7X_VMEM_BYTES - 8 * 1024 * 1024, nbytes * 3 // 2))


def _resident(block_shape, index_map):
    return pl.BlockSpec(block_shape, index_map, pipeline_mode=pl.Buffered(1))


def _layer_norm(y, g, b):
    mu = jnp.mean(y, axis=-1, keepdims=True)
    yc = y - mu
    var = jnp.mean(yc * yc, axis=-1, keepdims=True)
    return yc * lax.rsqrt(var + LN_EPS) * g + b


def _alibi_slopes(n):
    return jnp.asarray(2.0 ** (-8.0 * np.arange(1, n + 1) / n), dtype=F32)


def _ffn_ln_kernel(x_ref, wg_ref, wu_ref, w2_ref, g_ref, b_ref, o_ref, acc_ref):
    x = x_ref[...]
    xb = x.astype(BF16)
    for c in range(D_FF // FF_CHUNK):
        cols = slice(c * FF_CHUNK, (c + 1) * FF_CHUNK)
        gate = jnp.dot(xb, wg_ref[:, cols].astype(BF16), preferred_element_type=F32)
        up = jnp.dot(xb, wu_ref[:, cols].astype(BF16), preferred_element_type=F32)
        act = (gate / (1.0 + jnp.exp(-gate)) * up).astype(BF16)
        part = jnp.dot(act, w2_ref[cols, :].astype(BF16), preferred_element_type=F32)
        if c == 0:
            acc_ref[...] = part
        else:
            acc_ref[...] += part
    y = DN_ALPHA * x + 0.5 * acc_ref[...]
    o_ref[...] = _layer_norm(y, g_ref[...], b_ref[...])


def _ffn_ln(x, w13, w2, layer, g, b):
    n = x.shape[0]
    est = (2 * D_MODEL * D_FF + D_FF * D_MODEL) * 4 + 5 * FFN_TM * D_MODEL * 4 + 4 * FFN_TM * FF_CHUNK * 4
    return pl.pallas_call(
        _ffn_ln_kernel,
        out_shape=jax.ShapeDtypeStruct((n, D_MODEL), F32),
        grid=(n // FFN_TM,),
        in_specs=[
            pl.BlockSpec((FFN_TM, D_MODEL), lambda i: (i, 0)),
            _resident((None, D_MODEL, D_FF), lambda i: (layer, 0, 0)),
            _resident((None, D_MODEL, D_FF), lambda i: (layer, 0, 1)),
            _resident((None, D_FF, D_MODEL), lambda i: (layer, 0, 0)),
            _resident((1, D_MODEL), lambda i: (0, 0)),
            _resident((1, D_MODEL), lambda i: (0, 0)),
        ],
        out_specs=pl.BlockSpec((FFN_TM, D_MODEL), lambda i: (i, 0)),
        scratch_shapes=[pltpu.VMEM((FFN_TM, D_MODEL), F32)],
        compiler_params=pltpu.CompilerParams(
            dimension_semantics=("arbitrary",), vmem_limit_bytes=_vmem_limit(est)),
        name="ffn_ln",
    )(x, w13, w13, w2, g.reshape(1, D_MODEL), b.reshape(1, D_MODEL))


def _in_proj_a_kernel(x_ref, w_ref, q_ref, k_ref, vt_ref, qm_ref):
    xb = x_ref[...].astype(BF16)
    heads_per_chunk = PROJ_CHUNK // DIFF_V_DIM
    n_q = QK_A // PROJ_CHUNK
    n_qk = 2 * n_q
    n_v = TOK_WIDTH // PROJ_CHUNK
    for c in range(A_IN // PROJ_CHUNK):
        r = jnp.dot(xb, w_ref[:, c * PROJ_CHUNK:(c + 1) * PROJ_CHUNK], preferred_element_type=F32)
        if c < n_qk:
            dst, c0, scale = (q_ref, 0, QK_SCALE * LOG2E) if c < n_q else (k_ref, n_q, 1.0)
            for hh in range(heads_per_chunk):
                dst[(c - c0) * heads_per_chunk + hh] = (
                    r[:, hh * DIFF_V_DIM:(hh + 1) * DIFF_V_DIM] * scale).astype(BF16)
        elif c < n_qk + n_v:
            rt = r.T.astype(BF16)
            for hh in range(heads_per_chunk):
                for t in range(PROJ_TM // DIFF_TK):
                    vt_ref[(c - n_qk) * heads_per_chunk + hh, t] = rt[
                        hh * DIFF_V_DIM:(hh + 1) * DIFF_V_DIM, t * DIFF_TK:(t + 1) * DIFF_TK]
        else:
            qm_ref[...] = (r * QK_SCALE).astype(BF16)


def _in_proj_a(x, w_in, batch, seq):
    n = x.shape[0]
    w_in = w_in.astype(BF16)
    tiles_per_batch = seq // PROJ_TM
    chunks_per_tile = PROJ_TM // DIFF_TK
    est = D_MODEL * A_IN * 2 + 2 * PROJ_TM * D_MODEL * 4 + 2 * PROJ_TM * A_IN * 2 + 4 * PROJ_TM * PROJ_CHUNK * 4
    head_major = jax.ShapeDtypeStruct((batch, DIFF_HEADS, seq, DIFF_V_DIM), BF16)
    head_major_spec = pl.BlockSpec((None, DIFF_HEADS, PROJ_TM, DIFF_V_DIM),
                                   lambda i: (i // tiles_per_batch, 0, i % tiles_per_batch, 0))
    return pl.pallas_call(
        _in_proj_a_kernel,
        out_shape=(
            head_major,
            head_major,
            jax.ShapeDtypeStruct((batch, DIFF_HEADS, seq // DIFF_TK, DIFF_V_DIM, DIFF_TK), BF16),
            jax.ShapeDtypeStruct((n, MEM_WIDTH), BF16),
        ),
        grid=(n // PROJ_TM,),
        in_specs=[
            pl.BlockSpec((PROJ_TM, D_MODEL), lambda i: (i, 0)),
            _resident((D_MODEL, A_IN), lambda i: (0, 0)),
        ],
        out_specs=(
            head_major_spec,
            head_major_spec,
            pl.BlockSpec((None, DIFF_HEADS, chunks_per_tile, DIFF_V_DIM, DIFF_TK),
                         lambda i: (i // tiles_per_batch, 0, i % tiles_per_batch, 0, 0)),
            pl.BlockSpec((PROJ_TM, MEM_WIDTH), lambda i: (i, 0)),
        ),
        compiler_params=pltpu.CompilerParams(
            dimension_semantics=("arbitrary",), vmem_limit_bytes=_vmem_limit(est)),
        name="in_proj_a",
    )(x, w_in)


def _diff_attn_kernel(slopes_ref, q_ref, k_ref, vt_ref, lq1_ref, lk1_ref, lq2_ref, lk2_ref, g_ref,
                      o_ref, qt_ref, pos_ref, bias_ref, sr_ref, m_ref, l_ref, acc_ref,
                      st_ref, shift_ref, max_ref, p_ref, alpha_ref, *, lam_init, seq):
    h = pl.program_id(1)
    i = pl.program_id(2)
    tq, tk = DIFF_TQ, DIFF_TK
    assert tq == tk and tk <= 256
    n_steps = seq // tk
    n_strips = 2 * tq // V7X_LANES
    slope = slopes_ref[h]

    qt = q_ref[...].astype(F32).T
    row = lax.broadcasted_iota(jnp.int32, qt.shape, 0)
    zero = jnp.zeros_like(qt)
    qt_ref[0:DIFF_V_DIM, :tq] = jnp.where(row < HEAD_DIM, qt, zero).astype(BF16)
    qt_ref[0:DIFF_V_DIM, tq:] = jnp.where(row >= HEAD_DIM, qt, zero).astype(BF16)
    slope_v = jnp.full((16, 2 * tq), slope, F32)
    s_hi = slope_v.astype(BF16).astype(F32)
    s_mid = (slope_v - s_hi).astype(BF16).astype(F32)
    s_lo = slope_v - s_hi - s_mid
    prow = lax.broadcasted_iota(jnp.int32, (16, 2 * tq), 0)
    pieces = jnp.where(prow == 0, s_hi, jnp.where(prow == 1, s_mid, jnp.where(prow == 2, s_lo, 0.0)))
    qt_ref[DIFF_V_DIM:DIFF_V_DIM + 16, :] = pieces.astype(BF16)
    qt_ref[DIFF_V_DIM + 16:, :] = jnp.zeros((DIFF_V_DIM - 16, 2 * tq), BF16)

    r_s = lax.broadcasted_iota(jnp.int32, (tk, V7X_LANES), 0).astype(F32)
    lane = lax.broadcasted_iota(jnp.int32, (tk, V7X_LANES), 1)
    r_s = jnp.where(lane < 3, r_s, 0.0)
    pos_ref[0] = r_s.astype(BF16)
    pos_ref[1] = (-r_s).astype(BF16)
    pos_ref[2] = jnp.zeros((tk, V7X_LANES), BF16)

    rel = (lax.broadcasted_iota(jnp.int32, (tk, tq), 0) - lax.broadcasted_iota(jnp.int32, (tk, tq), 1))
    bias_ref[...] = slope * jnp.abs(rel).astype(F32)
    r_t = lax.broadcasted_iota(jnp.int32, (1, 2 * tq), 1)
    sr_ref[...] = slope * jnp.where(r_t >= tq, r_t - tq, r_t).astype(F32)

    m_ref[...] = jnp.full(m_ref.shape, -jnp.inf, F32)
    l_ref[...] = jnp.zeros(l_ref.shape, F32)
    acc_ref[...] = jnp.zeros(acc_ref.shape, F32)

    def keys(j):
        return k_ref[pl.ds(pl.multiple_of(j * tk, tk), tk), :]

    def other_tile(u):
        return jnp.where(u >= i, u + 1, u)

    def slot_tiles(s):
        first = i if (isinstance(s, int) and s == 0) else jnp.where(s == 0, i, other_tile(2 * s - 1))
        return first, other_tile(2 * s)

    def strip(c):
        return slice(c * V7X_LANES, (c + 1) * V7X_LANES)

    def scores(s, b):
        halves = []
        for e, j in enumerate(slot_tiles(s)):
            if isinstance(s, int) and s == 0 and e == 0:
                halves.append(jnp.concatenate([keys(j), pos_ref[2]], axis=1))
                shift_ref[b, e] = jnp.zeros((1, 2 * tq), F32)
            else:
                after = (j > i).astype(jnp.int32)
                halves.append(jnp.concatenate([keys(j), pos_ref[after]], axis=1))
                sign = (1 - 2 * after).astype(F32)
                tile_dist = (jnp.abs(j - i) * tk).astype(F32)
                shift_ref[b, e] = -(sign * sr_ref[...]) - slope * tile_dist
        st = jnp.dot(jnp.concatenate(halves, axis=0), qt_ref[...], preferred_element_type=F32)
        for e in range(2):
            for c in range(n_strips):
                blk = st[e * tk:(e + 1) * tk, strip(c)]
                if isinstance(s, int) and s == 0 and e == 0:
                    blk = blk - bias_ref[:, strip(c % (tq // V7X_LANES))]
                st_ref[b, e, c] = blk
                max_ref[b, e, :, strip(c)] = jnp.max(blk, axis=0, keepdims=True)

    def softmax(b):
        shift0, shift1 = shift_ref[b, 0], shift_ref[b, 1]
        m_old = m_ref[...]
        m_new = jnp.maximum(m_old, jnp.maximum(max_ref[b, 0] + shift0, max_ref[b, 1] + shift1))
        alpha = jnp.exp2(m_old - m_new)
        m_ref[...] = m_new
        alpha_ref[b] = alpha
        u0, u1 = m_new - shift0, m_new - shift1
        sums = []
        for c in range(n_strips):
            lanes = strip(c)
            p0 = jnp.exp2(st_ref[b, 0, c] - u0[:, lanes])
            p1 = jnp.exp2(st_ref[b, 1, c] - u1[:, lanes])
            sums.append(jnp.sum(p0, axis=0, keepdims=True) + jnp.sum(p1, axis=0, keepdims=True))
            p_ref[b, 0, c] = p0.astype(BF16)
            p_ref[b, 1, c] = p1.astype(BF16)
        l_ref[...] = alpha * l_ref[...] + jnp.concatenate(sums, axis=1)

    def pv(s, b):
        j0, j1 = slot_tiles(s)
        p = jnp.concatenate([jnp.concatenate([p_ref[b, e, c] for c in range(n_strips)], axis=1)
                             for e in range(2)], axis=0)
        v = jnp.concatenate([vt_ref[j0], vt_ref[j1]], axis=1)
        acc_ref[...] = alpha_ref[b] * acc_ref[...] + jnp.dot(v, p, preferred_element_type=F32)

    depth = DIFF_DEPTH
    n_slots = n_steps // 2
    assert n_steps % 2 == 0 and (n_slots - 2 * depth) % depth == 0

    def stage(s, r):
        if not (isinstance(s, int) and s < depth):
            pv(s - depth, r)
        softmax(r)
        if not (isinstance(s, int) and s + depth >= n_slots):
            scores(s + depth, r)

    for s in range(depth):
        scores(s, s)
    for s in range(depth):
        stage(s, s)

    def body(k, carry):
        for r in range(depth):
            stage(depth * (k + 1) + r, r)
        return carry

    lax.fori_loop(0, (n_slots - 2 * depth) // depth, body, 0)
    for s in range(n_slots - depth, n_slots):
        stage(s, s % depth)
    for s in range(n_slots - depth, n_slots):
        pv(s, s % depth)

    lam = (jnp.exp(jnp.sum(lq1_ref[...] * lk1_ref[...], axis=-1, keepdims=True))
           - jnp.exp(jnp.sum(lq2_ref[...] * lk2_ref[...], axis=-1, keepdims=True)) + lam_init)
    o_t = acc_ref[...] / l_ref[...]
    a = (o_t[:, :tq] - lam * o_t[:, tq:]).T
    a = a * lax.rsqrt(jnp.mean(a * a, axis=-1, keepdims=True) + LN_EPS) * g_ref[...]
    o_ref[...] = (a * (1.0 - lam_init)).astype(BF16)


def _diff_attn(q, k, vt, lq1, lk1, lq2, lk2, subln_g, lam_init, batch, seq):
    nq = seq // DIFF_TQ
    strips = (2 * DIFF_TQ // V7X_LANES, DIFF_TK, V7X_LANES)
    slopes = _alibi_slopes(DIFF_HEADS) * LOG2E
    vec = lambda v: v.reshape(1, -1).astype(F32)
    est = (2 * 2 * seq * DIFF_V_DIM * 2 + 4 * DIFF_TQ * DIFF_V_DIM * 2
           + DIFF_DEPTH * 2 * DIFF_TK * 2 * DIFF_TQ * (4 + 2)
           + 4 * DIFF_TK * 2 * DIFF_TQ * 4 + 2 * DIFF_V_DIM * 2 * DIFF_TQ * 4)
    small = lambda n: pl.BlockSpec((1, n), lambda b, h, i: (0, 0))
    return pl.pallas_call(
        functools.partial(_diff_attn_kernel, lam_init=lam_init, seq=seq),
        out_shape=jax.ShapeDtypeStruct((batch * seq, TOK_WIDTH), BF16),
        grid=(batch, DIFF_HEADS, nq),
        in_specs=[
            pl.BlockSpec(memory_space=pltpu.SMEM),
            pl.BlockSpec((None, None, DIFF_TQ, DIFF_V_DIM), lambda b, h, i: (b, h, i, 0)),
            pl.BlockSpec((None, None, seq, DIFF_V_DIM), lambda b, h, i: (b, h, 0, 0)),
            pl.BlockSpec((None, None, seq // DIFF_TK, DIFF_V_DIM, DIFF_TK), lambda b, h, i: (b, h, 0, 0, 0)),
            small(HEAD_DIM), small(HEAD_DIM), small(HEAD_DIM), small(HEAD_DIM), small(DIFF_V_DIM),
        ],
        out_specs=pl.BlockSpec((DIFF_TQ, DIFF_V_DIM), lambda b, h, i: (b * nq + i, h)),
        scratch_shapes=[
            pltpu.VMEM((2 * DIFF_V_DIM, 2 * DIFF_TQ), BF16),
            pltpu.VMEM((3, DIFF_TK, V7X_LANES), BF16),
            pltpu.VMEM((DIFF_TK, DIFF_TQ), F32),
            pltpu.VMEM((1, 2 * DIFF_TQ), F32),
            pltpu.VMEM((1, 2 * DIFF_TQ), F32),
            pltpu.VMEM((1, 2 * DIFF_TQ), F32),
            pltpu.VMEM((DIFF_V_DIM, 2 * DIFF_TQ), F32),
            pltpu.VMEM((DIFF_DEPTH, 2) + strips, F32),
            pltpu.VMEM((DIFF_DEPTH, 2, 1, 2 * DIFF_TQ), F32),
            pltpu.VMEM((DIFF_DEPTH, 2, 1, 2 * DIFF_TQ), F32),
            pltpu.VMEM((DIFF_DEPTH, 2) + strips, BF16),
            pltpu.VMEM((DIFF_DEPTH, 1, 2 * DIFF_TQ), F32),
        ],
        compiler_params=pltpu.CompilerParams(
            dimension_semantics=("arbitrary", "arbitrary", "arbitrary"), vmem_limit_bytes=_vmem_limit(est)),
        name="diff_attn",
    )(slopes, q, k, vt, vec(lq1), vec(lk1), vec(lq2), vec(lk2), vec(subln_g))


def _mem_kv_kernel(mem_ref, w_ref, kt_ref, v_ref):
    kv = jnp.dot(mem_ref[...].astype(BF16), w_ref[...], preferred_element_type=F32)
    kt = kv[:, :MEM_WIDTH].T
    v = kv[:, MEM_WIDTH:]
    pair_rows = 2 * HEAD_DIM
    row = lax.broadcasted_iota(jnp.int32, (pair_rows, N_MEM), 0)
    col = lax.broadcasted_iota(jnp.int32, (N_MEM, MEM_WIDTH), 1)
    for h in range(MEM_HEADS):
        pair = kt[(h // 2) * pair_rows:(h // 2 + 1) * pair_rows, :]
        keep = (row >= (h % 2) * HEAD_DIM) & (row < (h % 2 + 1) * HEAD_DIM)
        kt_ref[h] = jnp.where(keep, pair, jnp.zeros_like(pair)).astype(BF16)
        keep_v = (col >= h * HEAD_DIM) & (col < (h + 1) * HEAD_DIM)
        v_ref[h] = jnp.where(keep_v, v, jnp.zeros_like(v)).astype(BF16)


def _mem_kv(mem, w_mem_kv):
    batch = mem.shape[0]
    return pl.pallas_call(
        _mem_kv_kernel,
        out_shape=(
            jax.ShapeDtypeStruct((batch, MEM_HEADS, 2 * HEAD_DIM, N_MEM), BF16),
            jax.ShapeDtypeStruct((batch, MEM_HEADS, N_MEM, MEM_WIDTH), BF16),
        ),
        grid=(batch,),
        in_specs=[
            pl.BlockSpec((None, N_MEM, D_MODEL), lambda b: (b, 0, 0)),
            pl.BlockSpec((D_MODEL, 2 * MEM_WIDTH), lambda b: (0, 0)),
        ],
        out_specs=(
            pl.BlockSpec((None, MEM_HEADS, 2 * HEAD_DIM, N_MEM), lambda b: (b, 0, 0, 0)),
            pl.BlockSpec((None, MEM_HEADS, N_MEM, MEM_WIDTH), lambda b: (b, 0, 0, 0)),
        ),
        name="mem_kv",
    )(mem, w_mem_kv.astype(BF16))


def _mix_out_kernel(x_ref, o_ref, qm_ref, kt_ref, v_ref, wo_ref, wm_ref, g_ref, b_ref, out_ref):
    qm = qm_ref[...]
    pair_w = 2 * HEAD_DIM
    scores = [jnp.dot(qm[:, (h // 2) * pair_w:(h // 2 + 1) * pair_w], kt_ref[h],
                      preferred_element_type=F32) for h in range(MEM_HEADS)]
    exps = [jnp.exp(s - jnp.max(s, axis=-1, keepdims=True)) for s in scores]
    probs = [(e / jnp.sum(e, axis=-1, keepdims=True)).astype(BF16) for e in exps]
    mem_o = None
    for h in range(MEM_HEADS):
        part = jnp.dot(probs[h], v_ref[h], preferred_element_type=F32)
        mem_o = part if mem_o is None else mem_o + part
    y = jnp.dot(o_ref[...], wo_ref[...], preferred_element_type=F32)
    y = y + jnp.dot(mem_o.astype(BF16), wm_ref[...], preferred_element_type=F32)
    out_ref[...] = _layer_norm(DN_ALPHA * x_ref[...] + y, g_ref[...], b_ref[...])


def _mix_out(x, o, qm, kt, v, w_out, layer, g, b, seq):
    n = x.shape[0]
    tiles_per_batch = seq // MIX_TM
    est = (D_MODEL * D_MODEL * 2 + 4 * MIX_TM * D_MODEL * 4 + 2 * MIX_TM * D_MODEL * 2
           + 4 * MEM_HEADS * N_MEM * MEM_WIDTH * 2 + 6 * MIX_TM * D_MODEL * 4)
    return pl.pallas_call(
        _mix_out_kernel,
        out_shape=jax.ShapeDtypeStruct((n, D_MODEL), F32),
        grid=(n // MIX_TM,),
        in_specs=[
            pl.BlockSpec((MIX_TM, D_MODEL), lambda i: (i, 0)),
            pl.BlockSpec((MIX_TM, TOK_WIDTH), lambda i: (i, 0)),
            pl.BlockSpec((MIX_TM, MEM_WIDTH), lambda i: (i, 0)),
            pl.BlockSpec((None, MEM_HEADS, 2 * HEAD_DIM, N_MEM), lambda i: (i // tiles_per_batch, 0, 0, 0)),
            pl.BlockSpec((None, MEM_HEADS, N_MEM, MEM_WIDTH), lambda i: (i // tiles_per_batch, 0, 0, 0)),
            _resident((None, TOK_WIDTH, D_MODEL), lambda i: (layer, 0, 0)),
            _resident((None, MEM_WIDTH, D_MODEL), lambda i: (layer, TOK_WIDTH // MEM_WIDTH, 0)),
            _resident((1, D_MODEL), lambda i: (0, 0)),
            _resident((1, D_MODEL), lambda i: (0, 0)),
        ],
        out_specs=pl.BlockSpec((MIX_TM, D_MODEL), lambda i: (i, 0)),
        compiler_params=pltpu.CompilerParams(
            dimension_semantics=("arbitrary",), vmem_limit_bytes=_vmem_limit(est)),
        name="mix_out",
    )(x, o, qm, kt, v, w_out, w_out, g.reshape(1, D_MODEL), b.reshape(1, D_MODEL))


def _in_proj_b_kernel(x_ref, w_ref, q_ref, k_ref, vt_ref, qm_ref):
    assert WIN_KV_WIDTH == PROJ_CHUNK and MEM_WIDTH == PROJ_CHUNK
    xb = x_ref[...].astype(BF16)
    n_q = TOK_WIDTH // PROJ_CHUNK
    for c in range(B_IN // PROJ_CHUNK):
        cols = slice(c * PROJ_CHUNK, (c + 1) * PROJ_CHUNK)
        r = jnp.dot(xb, w_ref[:, cols], preferred_element_type=F32)
        if c < n_q:
            q_ref[:, cols] = (r * (QK_SCALE * LOG2E)).astype(BF16)
        elif c == n_q:
            k_ref[...] = r.astype(BF16)
        elif c == n_q + 1:
            rt = r.T.astype(BF16)
            for t in range(PROJ_TM // WINDOW):
                vt_ref[t] = rt[:, t * WINDOW:(t + 1) * WINDOW]
        else:
            qm_ref[...] = (r * QK_SCALE).astype(BF16)


def _in_proj_b(x, w_in, batch, seq):
    n = x.shape[0]
    tiles_per_batch = seq // PROJ_TM
    est = D_MODEL * B_IN * 2 + 2 * PROJ_TM * D_MODEL * 4 + 2 * PROJ_TM * B_IN * 2 + 4 * PROJ_TM * PROJ_CHUNK * 4
    return pl.pallas_call(
        _in_proj_b_kernel,
        out_shape=(
            jax.ShapeDtypeStruct((n, TOK_WIDTH), BF16),
            jax.ShapeDtypeStruct((n, WIN_KV_WIDTH), BF16),
            jax.ShapeDtypeStruct((batch, seq // WINDOW, WIN_KV_WIDTH, WINDOW), BF16),
            jax.ShapeDtypeStruct((n, MEM_WIDTH), BF16),
        ),
        grid=(n // PROJ_TM,),
        in_specs=[
            pl.BlockSpec((PROJ_TM, D_MODEL), lambda i: (i, 0)),
            _resident((D_MODEL, B_IN), lambda i: (0, 0)),
        ],
        out_specs=(
            pl.BlockSpec((PROJ_TM, TOK_WIDTH), lambda i: (i, 0)),
            pl.BlockSpec((PROJ_TM, WIN_KV_WIDTH), lambda i: (i, 0)),
            pl.BlockSpec((None, PROJ_TM // WINDOW, WIN_KV_WIDTH, WINDOW),
                         lambda i: (i // tiles_per_batch, i % tiles_per_batch, 0, 0)),
            pl.BlockSpec((PROJ_TM, MEM_WIDTH), lambda i: (i, 0)),
        ),
        compiler_params=pltpu.CompilerParams(
            dimension_semantics=("arbitrary",), vmem_limit_bytes=_vmem_limit(est)),
        name="in_proj_b",
    )(x, w_in.astype(BF16))


def _win_attn_kernel(slopes_ref, sink_ref, q_ref, k_ref, vt_ref, o_ref,
                     rel_ref, w_ref, st_ref, max_ref, p_ref, linv_ref, ot_ref, *, seq):
    i = pl.program_id(1)
    span = 3 * WINDOW
    nblk = WIN_TQ // WINDOW
    pair = 2 * HEAD_DIM
    assert WINDOW == V7X_LANES and nblk >= 4

    rel_ref[...] = (lax.broadcasted_iota(jnp.int32, (span, WINDOW), 0)
                    - lax.broadcasted_iota(jnp.int32, (span, WINDOW), 1)).astype(F32)

    def rows_of(n):
        return pl.ds(n * WINDOW, WINDOW) if isinstance(n, int) else pl.ds(pl.multiple_of(n * WINDOW, WINDOW), WINDOW)

    def window_of(n):
        gn = i * nblk + n
        return jnp.clip(gn - 1, 0, seq // WINDOW - 3), gn

    def prep_q(n):
        qt = q_ref[rows_of(n), :].astype(F32).T
        zeros = jnp.zeros((HEAD_DIM, WINDOW), F32)
        for g in range(WIN_KV_HEADS):
            cols = []
            for hh in range(WIN_GROUP):
                x = qt[(g * WIN_GROUP + hh) * HEAD_DIM:(g * WIN_GROUP + hh + 1) * HEAD_DIM, :]
                cols.append(jnp.concatenate([x, zeros] if g % 2 == 0 else [zeros, x], axis=0))
            w_ref[g] = jnp.concatenate(cols, axis=1).astype(BF16)

    def scores(n, g):
        c0, gn = window_of(n)
        keys = k_ref[pl.ds(pl.multiple_of(c0 * WINDOW, WINDOW), span), (g // 2) * pair:(g // 2 + 1) * pair]
        st = jnp.dot(keys, w_ref[g], preferred_element_type=F32)
        dist = jnp.abs(((gn - c0) * WINDOW).astype(F32) - rel_ref[...])
        penalty = jnp.where(dist <= WINDOW, 0.0, -NEG)
        for hh in range(WIN_GROUP):
            head = g * WIN_GROUP + hh
            lanes = slice(hh * WINDOW, (hh + 1) * WINDOW)
            s = st[:, lanes] - (slopes_ref[head] * dist + penalty)
            st_ref[g, :, lanes] = s
            max_ref[g, :, lanes] = jnp.maximum(jnp.max(s, axis=0, keepdims=True), sink_ref[head])

    def softmax(n, g):
        for hh in range(WIN_GROUP):
            head = g * WIN_GROUP + hh
            lanes = slice(hh * WINDOW, (hh + 1) * WINDOW)
            m = max_ref[g, :, lanes]
            p = jnp.exp2(st_ref[g, :, lanes] - m)
            p_ref[g, :, lanes] = p.astype(BF16)
            linv_ref[g, :, lanes] = 1.0 / (jnp.sum(p, axis=0, keepdims=True) + jnp.exp2(sink_ref[head] - m))

    def pv(n, g):
        c0, _ = window_of(n)
        vt = jnp.concatenate([vt_ref[c0 + e, g * HEAD_DIM:(g + 1) * HEAD_DIM, :] for e in range(3)], axis=1)
        ot_ref[g] = jnp.dot(vt, p_ref[g], preferred_element_type=F32) * linv_ref[g]

    def assemble(n):
        for pp in range(WIN_Q_HEADS // 2):
            parts = []
            for head in (2 * pp, 2 * pp + 1):
                g, hh = divmod(head, WIN_GROUP)
                parts.append(ot_ref[g, :, hh * WINDOW:(hh + 1) * WINDOW])
            o_ref[rows_of(n), pp * pair:(pp + 1) * pair] = jnp.concatenate(parts, axis=0).T.astype(BF16)

    def iteration(n, do_assemble, do_pv, do_softmax, do_scores):
        nxt = min(n + 1, nblk - 1) if isinstance(n, int) else jnp.minimum(n + 1, nblk - 1)
        if do_assemble:
            assemble(n - 2)
        if do_scores:
            prep_q(nxt)
        for g in range(WIN_KV_HEADS):
            if do_pv:
                pv(n - 1, g)
            if do_softmax:
                softmax(n, g)
            if do_scores:
                scores(nxt, g)

    prep_q(0)
    for g in range(WIN_KV_HEADS):
        scores(0, g)
    iteration(0, False, False, True, True)
    iteration(1, False, True, True, True)

    def body(n, carry):
        iteration(n, True, True, True, True)
        return carry

    lax.fori_loop(2, nblk, body, 0)
    iteration(nblk, True, True, False, False)
    iteration(nblk + 1, True, False, False, False)


def _win_attn(q, k, vt, sink, batch, seq):
    nq = seq // WIN_TQ
    span = 3 * WINDOW
    slopes = _alibi_slopes(WIN_Q_HEADS) * LOG2E
    est = (2 * 2 * seq * WIN_KV_WIDTH * 2 + 4 * WIN_TQ * TOK_WIDTH * 2
           + WIN_KV_HEADS * span * span * (4 + 2) + 8 * span * span * 4)
    return pl.pallas_call(
        functools.partial(_win_attn_kernel, seq=seq),
        out_shape=jax.ShapeDtypeStruct((batch * seq, TOK_WIDTH), BF16),
        grid=(batch, nq),
        in_specs=[
            pl.BlockSpec(memory_space=pltpu.SMEM),
            pl.BlockSpec(memory_space=pltpu.SMEM),
            pl.BlockSpec((WIN_TQ, TOK_WIDTH), lambda b, i: (b * nq + i, 0)),
            pl.BlockSpec((seq, WIN_KV_WIDTH), lambda b, i: (b, 0)),
            pl.BlockSpec((None, seq // WINDOW, WIN_KV_WIDTH, WINDOW), lambda b, i: (b, 0, 0, 0)),
        ],
        out_specs=pl.BlockSpec((WIN_TQ, TOK_WIDTH), lambda b, i: (b * nq + i, 0)),
        scratch_shapes=[
            pltpu.VMEM((span, WINDOW), F32),
            pltpu.VMEM((WIN_KV_HEADS, 2 * HEAD_DIM, span), BF16),
            pltpu.VMEM((WIN_KV_HEADS, span, span), F32),
            pltpu.VMEM((WIN_KV_HEADS, 1, span), F32),
            pltpu.VMEM((WIN_KV_HEADS, span, span), BF16),
            pltpu.VMEM((WIN_KV_HEADS, 1, span), F32),
            pltpu.VMEM((WIN_KV_HEADS, HEAD_DIM, span), F32),
        ],
        compiler_params=pltpu.CompilerParams(
            dimension_semantics=("arbitrary", "arbitrary"), vmem_limit_bytes=_vmem_limit(est)),
        name="win_attn",
    )(slopes, sink.astype(F32) * LOG2E, q, k, vt)


def kernel(x, mem, ffn1_w13, ffn1_w2, ln1_g, ln1_b, w_mem_kv, w_out, ln2_g, ln2_b, ffn2_w13, ffn2_w2,
           ln3_g, ln3_b, a_w_in, a_lambda_q1, a_lambda_k1, a_lambda_q2, a_lambda_k2, a_subln_g, b_w_in, b_sink):
    batch, seq, _ = x.shape
    xf = x.reshape(batch * seq, D_MODEL)
    w_out = w_out.astype(BF16)
    for i in range(DEPTH):
        xf = _ffn_ln(xf, ffn1_w13, ffn1_w2, i, ln1_g[i], ln1_b[i])
        j = i // N_MIXERS
        kt, v = _mem_kv(mem, w_mem_kv[i])
        if i % N_MIXERS == 0:
            lam_init = 0.8 - 0.6 * math.exp(-0.3 * i)
            q, k, vt, qm = _in_proj_a(xf, a_w_in[j], batch, seq)
            o = _diff_attn(q, k, vt, a_lambda_q1[j], a_lambda_k1[j], a_lambda_q2[j], a_lambda_k2[j],
                           a_subln_g[j], lam_init, batch, seq)
        else:
            q, k, vt, qm = _in_proj_b(xf, b_w_in[j], batch, seq)
            o = _win_attn(q, k, vt, b_sink[j], batch, seq)
        xf = _mix_out(xf, o, qm, kt, v, w_out, i, ln2_g[i], ln2_b[i], seq)
        xf = _ffn_ln(xf, ffn2_w13, ffn2_w2, i, ln3_g[i], ln3_b[i])
    return xf.reshape(batch, seq, D_MODEL)
```

```python
import functools
import math

import numpy as np
import jax
import jax.numpy as jnp
from jax import lax
from jax.experimental import pallas as pl
from jax.experimental.pallas import tpu as pltpu

D_MODEL = 1024
DEPTH = 2
N_MIXERS = 2
N_MEM = 256
HEAD_DIM = 64
MEM_HEADS = 4
MEM_WIDTH = MEM_HEADS * HEAD_DIM
TOK_WIDTH = D_MODEL - MEM_WIDTH
DIFF_HEADS = TOK_WIDTH // (2 * HEAD_DIM)
DIFF_V_DIM = 2 * HEAD_DIM
QK_A = DIFF_HEADS * 2 * HEAD_DIM
WIN_Q_HEADS = TOK_WIDTH // HEAD_DIM
WIN_GROUP = 3
WIN_KV_HEADS = WIN_Q_HEADS // WIN_GROUP
WIN_KV_WIDTH = WIN_KV_HEADS * HEAD_DIM
WINDOW = 128
D_FF = 2816
A_IN = 2 * QK_A + TOK_WIDTH + MEM_WIDTH
B_IN = TOK_WIDTH + 2 * WIN_KV_WIDTH + MEM_WIDTH
N_A = (DEPTH + 1) // 2
DN_ALPHA = (2 * DEPTH) ** 0.25
LN_EPS = 1e-5
NEG = -1e30
QK_SCALE = HEAD_DIM ** -0.5
LOG2E = math.log2(math.e)

F32 = jnp.float32
BF16 = jnp.bfloat16

V7X_LANES = 128
V7X_MXU_DIM = 256
V7X_VMEM_BYTES = 64 * 1024 * 1024

FFN_TM = 512
FF_CHUNK = V7X_MXU_DIM
PROJ_TM = 1024
PROJ_CHUNK = V7X_MXU_DIM
DIFF_TQ = 256
DIFF_TK = 256
DIFF_TILES = 4
DIFF_DEPTH = 4
MIX_TM = 512
WIN_TQ = 1024


def _vmem_limit(nbytes):
    return int(min(V7X_VMEM_BYTES - 8 * 1024 * 1024, nbytes * 3 // 2))


def _resident(block_shape, index_map):
    return pl.BlockSpec(block_shape, index_map, pipeline_mode=pl.Buffered(1))


def _layer_norm(y, g, b):
    mu = jnp.mean(y, axis=-1, keepdims=True)
    yc = y - mu
    var = jnp.mean(yc * yc, axis=-1, keepdims=True)
    return yc * lax.rsqrt(var + LN_EPS) * g + b


def _alibi_slopes(n):
    return jnp.asarray(2.0 ** (-8.0 * np.arange(1, n + 1) / n), dtype=F32)


def _ffn_ln_kernel(x_ref, wg_ref, wu_ref, w2_ref, g_ref, b_ref, o_ref, acc_ref):
    x = x_ref[...]
    xb = x.astype(BF16)
    for c in range(D_FF // FF_CHUNK):
        cols = slice(c * FF_CHUNK, (c + 1) * FF_CHUNK)
        gate = jnp.dot(xb, wg_ref[:, cols].astype(BF16), preferred_element_type=F32)
        up = jnp.dot(xb, wu_ref[:, cols].astype(BF16), preferred_element_type=F32)
        act = (gate / (1.0 + jnp.exp(-gate)) * up).astype(BF16)
        part = jnp.dot(act, w2_ref[cols, :].astype(BF16), preferred_element_type=F32)
        if c == 0:
            acc_ref[...] = part
        else:
            acc_ref[...] += part
    y = DN_ALPHA * x + 0.5 * acc_ref[...]
    o_ref[...] = _layer_norm(y, g_ref[...], b_ref[...])


def _ffn_ln(x, w13, w2, layer, g, b):
    n = x.shape[0]
    est = (2 * D_MODEL * D_FF + D_FF * D_MODEL) * 4 + 5 * FFN_TM * D_MODEL * 4 + 4 * FFN_TM * FF_CHUNK * 4
    return pl.pallas_call(
        _ffn_ln_kernel,
        out_shape=jax.ShapeDtypeStruct((n, D_MODEL), F32),
        grid=(n // FFN_TM,),
        in_specs=[
            pl.BlockSpec((FFN_TM, D_MODEL), lambda i: (i, 0)),
            _resident((None, D_MODEL, D_FF), lambda i: (layer, 0, 0)),
            _resident((None, D_MODEL, D_FF), lambda i: (layer, 0, 1)),
            _resident((None, D_FF, D_MODEL), lambda i: (layer, 0, 0)),
            _resident((1, D_MODEL), lambda i: (0, 0)),
            _resident((1, D_MODEL), lambda i: (0, 0)),
        ],
        out_specs=pl.BlockSpec((FFN_TM, D_MODEL), lambda i: (i, 0)),
        scratch_shapes=[pltpu.VMEM((FFN_TM, D_MODEL), F32)],
        compiler_params=pltpu.CompilerParams(
            dimension_semantics=("arbitrary",), vmem_limit_bytes=_vmem_limit(est)),
        name="ffn_ln",
    )(x, w13, w13, w2, g.reshape(1, D_MODEL), b.reshape(1, D_MODEL))


def _in_proj_a_kernel(x_ref, w_ref, q_ref, k_ref, vt_ref, qm_ref):
    xb = x_ref[...].astype(BF16)
    heads_per_chunk = PROJ_CHUNK // DIFF_V_DIM
    n_q = QK_A // PROJ_CHUNK
    n_qk = 2 * n_q
    n_v = TOK_WIDTH // PROJ_CHUNK
    for c in range(A_IN // PROJ_CHUNK):
        r = jnp.dot(xb, w_ref[:, c * PROJ_CHUNK:(c + 1) * PROJ_CHUNK], preferred_element_type=F32)
        if c < n_qk:
            dst, c0, scale = (q_ref, 0, QK_SCALE * LOG2E) if c < n_q else (k_ref, n_q, 1.0)
            for hh in range(heads_per_chunk):
                dst[(c - c0) * heads_per_chunk + hh] = (
                    r[:, hh * DIFF_V_DIM:(hh + 1) * DIFF_V_DIM] * scale).astype(BF16)
        elif c < n_qk + n_v:
            rt = r.T.astype(BF16)
            for hh in range(heads_per_chunk):
                for t in range(PROJ_TM // DIFF_TK):
                    vt_ref[(c - n_qk) * heads_per_chunk + hh, t] = rt[
                        hh * DIFF_V_DIM:(hh + 1) * DIFF_V_DIM, t * DIFF_TK:(t + 1) * DIFF_TK]
        else:
            qm_ref[...] = (r * QK_SCALE).astype(BF16)


def _in_proj_a(x, w_in, batch, seq):
    n = x.shape[0]
    w_in = w_in.astype(BF16)
    tiles_per_batch = seq // PROJ_TM
    chunks_per_tile = PROJ_TM // DIFF_TK
    est = D_MODEL * A_IN * 2 + 2 * PROJ_TM * D_MODEL * 4 + 2 * PROJ_TM * A_IN * 2 + 4 * PROJ_TM * PROJ_CHUNK * 4
    head_major = jax.ShapeDtypeStruct((batch, DIFF_HEADS, seq, DIFF_V_DIM), BF16)
    head_major_spec = pl.BlockSpec((None, DIFF_HEADS, PROJ_TM, DIFF_V_DIM),
                                   lambda i: (i // tiles_per_batch, 0, i % tiles_per_batch, 0))
    return pl.pallas_call(
        _in_proj_a_kernel,
        out_shape=(
            head_major,
            head_major,
            jax.ShapeDtypeStruct((batch, DIFF_HEADS, seq // DIFF_TK, DIFF_V_DIM, DIFF_TK), BF16),
            jax.ShapeDtypeStruct((n, MEM_WIDTH), BF16),
        ),
        grid=(n // PROJ_TM,),
        in_specs=[
            pl.BlockSpec((PROJ_TM, D_MODEL), lambda i: (i, 0)),
            _resident((D_MODEL, A_IN), lambda i: (0, 0)),
        ],
        out_specs=(
            head_major_spec,
            head_major_spec,
            pl.BlockSpec((None, DIFF_HEADS, chunks_per_tile, DIFF_V_DIM, DIFF_TK),
                         lambda i: (i // tiles_per_batch, 0, i % tiles_per_batch, 0, 0)),
            pl.BlockSpec((PROJ_TM, MEM_WIDTH), lambda i: (i, 0)),
        ),
        compiler_params=pltpu.CompilerParams(
            dimension_semantics=("arbitrary",), vmem_limit_bytes=_vmem_limit(est)),
        name="in_proj_a",
    )(x, w_in)


def _diff_attn_kernel(slopes_ref, q_ref, k_ref, vt_ref, lq1_ref, lk1_ref, lq2_ref, lk2_ref, g_ref,
                      o_ref, qt_ref, pos_ref, bias_ref, sr_ref, m_ref, l_ref, acc_ref,
                      st_ref, shift_ref, max_ref, p_ref, alpha_ref, *, lam_init, seq):
    h = pl.program_id(1)
    tq, tk = DIFF_TQ, DIFF_TK
    assert tq == tk and tk <= 256
    n_steps = seq // tk
    n_strips = 2 * tq // V7X_LANES
    slope = slopes_ref[h]
    tiles = [pl.program_id(2) * DIFF_TILES + u for u in range(DIFF_TILES)]

    slope_v = jnp.full((16, 2 * tq), slope, F32)
    s_hi = slope_v.astype(BF16).astype(F32)
    s_mid = (slope_v - s_hi).astype(BF16).astype(F32)
    s_lo = slope_v - s_hi - s_mid
    prow = lax.broadcasted_iota(jnp.int32, (16, 2 * tq), 0)
    pieces = jnp.where(prow == 0, s_hi, jnp.where(prow == 1, s_mid, jnp.where(prow == 2, s_lo, 0.0)))
    for u in range(DIFF_TILES):
        qt = q_ref[u * tq:(u + 1) * tq, :].astype(F32).T
        row = lax.broadcasted_iota(jnp.int32, qt.shape, 0)
        zero = jnp.zeros_like(qt)
        qt_ref[u, 0:DIFF_V_DIM, :tq] = jnp.where(row < HEAD_DIM, qt, zero).astype(BF16)
        qt_ref[u, 0:DIFF_V_DIM, tq:] = jnp.where(row >= HEAD_DIM, qt, zero).astype(BF16)
        qt_ref[u, DIFF_V_DIM:DIFF_V_DIM + 16, :] = pieces.astype(BF16)
        qt_ref[u, DIFF_V_DIM + 16:, :] = jnp.zeros((DIFF_V_DIM - 16, 2 * tq), BF16)

    r_s = lax.broadcasted_iota(jnp.int32, (tk, V7X_LANES), 0).astype(F32)
    lane = lax.broadcasted_iota(jnp.int32, (tk, V7X_LANES), 1)
    r_s = jnp.where(lane < 3, r_s, 0.0)
    pos_ref[0] = r_s.astype(BF16)
    pos_ref[1] = (-r_s).astype(BF16)
    pos_ref[2] = jnp.zeros((tk, V7X_LANES), BF16)

    rel = (lax.broadcasted_iota(jnp.int32, (tk, tq), 0) - lax.broadcasted_iota(jnp.int32, (tk, tq), 1))
    bias_ref[...] = slope * jnp.abs(rel).astype(F32)
    r_t = lax.broadcasted_iota(jnp.int32, (1, 2 * tq), 1)
    sr_ref[...] = slope * jnp.where(r_t >= tq, r_t - tq, r_t).astype(F32)

    m_ref[...] = jnp.full(m_ref.shape, -jnp.inf, F32)
    l_ref[...] = jnp.zeros(l_ref.shape, F32)
    acc_ref[...] = jnp.zeros(acc_ref.shape, F32)

    def keys(j):
        return k_ref[pl.ds(pl.multiple_of(j * tk, tk), tk), :]

    def slot_tiles(s, u):
        i = tiles[u]
        other = lambda x: jnp.where(x >= i, x + 1, x)
        first = i if (isinstance(s, int) and s == 0) else jnp.where(s == 0, i, other(2 * s - 1))
        return first, other(2 * s)

    def strip(c):
        return slice(c * V7X_LANES, (c + 1) * V7X_LANES)

    def scores(s, b, u):
        i = tiles[u]
        halves = []
        for e, j in enumerate(slot_tiles(s, u)):
            if isinstance(s, int) and s == 0 and e == 0:
                halves.append(jnp.concatenate([keys(j), pos_ref[2]], axis=1))
                shift_ref[b, e] = jnp.zeros((1, 2 * tq), F32)
            else:
                after = (j > i).astype(jnp.int32)
                halves.append(jnp.concatenate([keys(j), pos_ref[after]], axis=1))
                sign = (1 - 2 * after).astype(F32)
                tile_dist = (jnp.abs(j - i) * tk).astype(F32)
                shift_ref[b, e] = -(sign * sr_ref[...]) - slope * tile_dist
        st = jnp.dot(jnp.concatenate(halves, axis=0), qt_ref[u], preferred_element_type=F32)
        for e in range(2):
            for c in range(n_strips):
                blk = st[e * tk:(e + 1) * tk, strip(c)]
                if isinstance(s, int) and s == 0 and e == 0:
                    blk = blk - bias_ref[:, strip(c % (tq // V7X_LANES))]
                st_ref[b, e, c] = blk
                max_ref[b, e, :, strip(c)] = jnp.max(blk, axis=0, keepdims=True)

    def softmax(b, u):
        shift0, shift1 = shift_ref[b, 0], shift_ref[b, 1]
        m_old = m_ref[u]
        m_new = jnp.maximum(m_old, jnp.maximum(max_ref[b, 0] + shift0, max_ref[b, 1] + shift1))
        alpha = jnp.exp2(m_old - m_new)
        m_ref[u] = m_new
        alpha_ref[b] = alpha
        u0, u1 = m_new - shift0, m_new - shift1
        sums = []
        for c in range(n_strips):
            lanes = strip(c)
            p0 = jnp.exp2(st_ref[b, 0, c] - u0[:, lanes])
            p1 = jnp.exp2(st_ref[b, 1, c] - u1[:, lanes])
            sums.append(jnp.sum(p0, axis=0, keepdims=True) + jnp.sum(p1, axis=0, keepdims=True))
            p_ref[b, 0, c] = p0.astype(BF16)
            p_ref[b, 1, c] = p1.astype(BF16)
        l_ref[u] = alpha * l_ref[u] + jnp.concatenate(sums, axis=1)

    def pv(s, b, u):
        j0, j1 = slot_tiles(s, u)
        p = jnp.concatenate([jnp.concatenate([p_ref[b, e, c] for c in range(n_strips)], axis=1)
                             for e in range(2)], axis=0)
        v = jnp.concatenate([vt_ref[j0], vt_ref[j1]], axis=1)
        acc_ref[u] = alpha_ref[b] * acc_ref[u] + jnp.dot(v, p, preferred_element_type=F32)

    lam = (jnp.exp(jnp.sum(lq1_ref[...] * lk1_ref[...], axis=-1, keepdims=True))
           - jnp.exp(jnp.sum(lq2_ref[...] * lk2_ref[...], axis=-1, keepdims=True)) + lam_init)

    def finalize(u):
        o_t = acc_ref[u] / l_ref[u]
        a = (o_t[:, :tq] - lam * o_t[:, tq:]).T
        a = a * lax.rsqrt(jnp.mean(a * a, axis=-1, keepdims=True) + LN_EPS) * g_ref[...]
        o_ref[u * tq:(u + 1) * tq, :] = (a * (1.0 - lam_init)).astype(BF16)

    depth = DIFF_DEPTH
    n_slots = n_steps // 2
    assert n_steps % 2 == 0 and (n_slots - 2 * depth) % depth == 0

    def stage(s, r, u):
        if not (isinstance(s, int) and s < depth):
            pv(s - depth, r, u)
        elif u > 0:
            pv(n_slots - depth + s, r, u - 1)
        softmax(r, u)
        if not (isinstance(s, int) and s + depth >= n_slots):
            scores(s + depth, r, u)
        elif u + 1 < DIFF_TILES:
            scores(s + depth - n_slots, r, u + 1)

    for s in range(depth):
        scores(s, s, 0)
    for u in range(DIFF_TILES):
        for s in range(depth):
            stage(s, s, u)
        if u > 0:
            finalize(u - 1)

        def body(k, carry, u=u):
            for r in range(depth):
                stage(depth * (k + 1) + r, r, u)
            return carry

        lax.fori_loop(0, (n_slots - 2 * depth) // depth, body, 0)
        for s in range(n_slots - depth, n_slots):
            stage(s, s % depth, u)
    for s in range(n_slots - depth, n_slots):
        pv(s, s % depth, DIFF_TILES - 1)
    finalize(DIFF_TILES - 1)


def _diff_attn(q, k, vt, lq1, lk1, lq2, lk2, subln_g, lam_init, batch, seq):
    nq = seq // (DIFF_TILES * DIFF_TQ)
    rows = DIFF_TILES * DIFF_TQ
    strips = (2 * DIFF_TQ // V7X_LANES, DIFF_TK, V7X_LANES)
    slopes = _alibi_slopes(DIFF_HEADS) * LOG2E
    vec = lambda v: v.reshape(1, -1).astype(F32)
    est = (2 * 2 * seq * DIFF_V_DIM * 2 + 4 * rows * DIFF_V_DIM * 2
           + DIFF_TILES * 3 * DIFF_V_DIM * 2 * DIFF_TQ * 4
           + DIFF_DEPTH * 2 * DIFF_TK * 2 * DIFF_TQ * (4 + 2)
           + 4 * DIFF_TK * 2 * DIFF_TQ * 4 + 2 * DIFF_V_DIM * 2 * DIFF_TQ * 4)
    small = lambda n: pl.BlockSpec((1, n), lambda b, h, i: (0, 0))
    return pl.pallas_call(
        functools.partial(_diff_attn_kernel, lam_init=lam_init, seq=seq),
        out_shape=jax.ShapeDtypeStruct((batch * seq, TOK_WIDTH), BF16),
        grid=(batch, DIFF_HEADS, nq),
        in_specs=[
            pl.BlockSpec(memory_space=pltpu.SMEM),
            pl.BlockSpec((None, None, rows, DIFF_V_DIM), lambda b, h, i: (b, h, i, 0)),
            pl.BlockSpec((None, None, seq, DIFF_V_DIM), lambda b, h, i: (b, h, 0, 0)),
            pl.BlockSpec((None, None, seq // DIFF_TK, DIFF_V_DIM, DIFF_TK), lambda b, h, i: (b, h, 0, 0, 0)),
            small(HEAD_DIM), small(HEAD_DIM), small(HEAD_DIM), small(HEAD_DIM), small(DIFF_V_DIM),
        ],
        out_specs=pl.BlockSpec((rows, DIFF_V_DIM), lambda b, h, i: (b * nq + i, h)),
        scratch_shapes=[
            pltpu.VMEM((DIFF_TILES, 2 * DIFF_V_DIM, 2 * DIFF_TQ), BF16),
            pltpu.VMEM((3, DIFF_TK, V7X_LANES), BF16),
            pltpu.VMEM((DIFF_TK, DIFF_TQ), F32),
            pltpu.VMEM((1, 2 * DIFF_TQ), F32),
            pltpu.VMEM((DIFF_TILES, 1, 2 * DIFF_TQ), F32),
            pltpu.VMEM((DIFF_TILES, 1, 2 * DIFF_TQ), F32),
            pltpu.VMEM((DIFF_TILES, DIFF_V_DIM, 2 * DIFF_TQ), F32),
            pltpu.VMEM((DIFF_DEPTH, 2) + strips, F32),
            pltpu.VMEM((DIFF_DEPTH, 2, 1, 2 * DIFF_TQ), F32),
            pltpu.VMEM((DIFF_DEPTH, 2, 1, 2 * DIFF_TQ), F32),
            pltpu.VMEM((DIFF_DEPTH, 2) + strips, BF16),
            pltpu.VMEM((DIFF_DEPTH, 1, 2 * DIFF_TQ), F32),
        ],
        compiler_params=pltpu.CompilerParams(
            dimension_semantics=("arbitrary", "arbitrary", "arbitrary"), vmem_limit_bytes=_vmem_limit(est)),
        name="diff_attn",
    )(slopes, q, k, vt, vec(lq1), vec(lk1), vec(lq2), vec(lk2), vec(subln_g))


def _mem_kv_kernel(mem_ref, w_ref, kt_ref, v_ref):
    kv = jnp.dot(mem_ref[...].astype(BF16), w_ref[...], preferred_element_type=F32)
    kt = kv[:, :MEM_WIDTH].T
    v = kv[:, MEM_WIDTH:]
    pair_rows = 2 * HEAD_DIM
    row = lax.broadcasted_iota(jnp.int32, (pair_rows, N_MEM), 0)
    col = lax.broadcasted_iota(jnp.int32, (N_MEM, MEM_WIDTH), 1)
    for h in range(MEM_HEADS):
        pair = kt[(h // 2) * pair_rows:(h // 2 + 1) * pair_rows, :]
        keep = (row >= (h % 2) * HEAD_DIM) & (row < (h % 2 + 1) * HEAD_DIM)
        kt_ref[h] = jnp.where(keep, pair, jnp.zeros_like(pair)).astype(BF16)
        keep_v = (col >= h * HEAD_DIM) & (col < (h + 1) * HEAD_DIM)
        v_ref[h] = jnp.where(keep_v, v, jnp.zeros_like(v)).astype(BF16)


def _mem_kv(mem, w_mem_kv):
    batch = mem.shape[0]
    return pl.pallas_call(
        _mem_kv_kernel,
        out_shape=(
            jax.ShapeDtypeStruct((batch, MEM_HEADS, 2 * HEAD_DIM, N_MEM), BF16),
            jax.ShapeDtypeStruct((batch, MEM_HEADS, N_MEM, MEM_WIDTH), BF16),
        ),
        grid=(batch,),
        in_specs=[
            pl.BlockSpec((None, N_MEM, D_MODEL), lambda b: (b, 0, 0)),
            pl.BlockSpec((D_MODEL, 2 * MEM_WIDTH), lambda b: (0, 0)),
        ],
        out_specs=(
            pl.BlockSpec((None, MEM_HEADS, 2 * HEAD_DIM, N_MEM), lambda b: (b, 0, 0, 0)),
            pl.BlockSpec((None, MEM_HEADS, N_MEM, MEM_WIDTH), lambda b: (b, 0, 0, 0)),
        ),
        name="mem_kv",
    )(mem, w_mem_kv.astype(BF16))


def _mix_out_kernel(x_ref, o_ref, qm_ref, kt_ref, v_ref, wo_ref, wm_ref, g_ref, b_ref, out_ref):
    qm = qm_ref[...]
    pair_w = 2 * HEAD_DIM
    scores = [jnp.dot(qm[:, (h // 2) * pair_w:(h // 2 + 1) * pair_w], kt_ref[h],
                      preferred_element_type=F32) for h in range(MEM_HEADS)]
    exps = [jnp.exp(s - jnp.max(s, axis=-1, keepdims=True)) for s in scores]
    probs = [(e / jnp.sum(e, axis=-1, keepdims=True)).astype(BF16) for e in exps]
    mem_o = None
    for h in range(MEM_HEADS):
        part = jnp.dot(probs[h], v_ref[h], preferred_element_type=F32)
        mem_o = part if mem_o is None else mem_o + part
    y = jnp.dot(o_ref[...], wo_ref[...], preferred_element_type=F32)
    y = y + jnp.dot(mem_o.astype(BF16), wm_ref[...], preferred_element_type=F32)
    out_ref[...] = _layer_norm(DN_ALPHA * x_ref[...] + y, g_ref[...], b_ref[...])


def _mix_out(x, o, qm, kt, v, w_out, layer, g, b, seq):
    n = x.shape[0]
    tiles_per_batch = seq // MIX_TM
    est = (D_MODEL * D_MODEL * 2 + 4 * MIX_TM * D_MODEL * 4 + 2 * MIX_TM * D_MODEL * 2
           + 4 * MEM_HEADS * N_MEM * MEM_WIDTH * 2 + 6 * MIX_TM * D_MODEL * 4)
    return pl.pallas_call(
        _mix_out_kernel,
        out_shape=jax.ShapeDtypeStruct((n, D_MODEL), F32),
        grid=(n // MIX_TM,),
        in_specs=[
            pl.BlockSpec((MIX_TM, D_MODEL), lambda i: (i, 0)),
            pl.BlockSpec((MIX_TM, TOK_WIDTH), lambda i: (i, 0)),
            pl.BlockSpec((MIX_TM, MEM_WIDTH), lambda i: (i, 0)),
            pl.BlockSpec((None, MEM_HEADS, 2 * HEAD_DIM, N_MEM), lambda i: (i // tiles_per_batch, 0, 0, 0)),
            pl.BlockSpec((None, MEM_HEADS, N_MEM, MEM_WIDTH), lambda i: (i // tiles_per_batch, 0, 0, 0)),
            _resident((None, TOK_WIDTH, D_MODEL), lambda i: (layer, 0, 0)),
            _resident((None, MEM_WIDTH, D_MODEL), lambda i: (layer, TOK_WIDTH // MEM_WIDTH, 0)),
            _resident((1, D_MODEL), lambda i: (0, 0)),
            _resident((1, D_MODEL), lambda i: (0, 0)),
        ],
        out_specs=pl.BlockSpec((MIX_TM, D_MODEL), lambda i: (i, 0)),
        compiler_params=pltpu.CompilerParams(
            dimension_semantics=("arbitrary",), vmem_limit_bytes=_vmem_limit(est)),
        name="mix_out",
    )(x, o, qm, kt, v, w_out, w_out, g.reshape(1, D_MODEL), b.reshape(1, D_MODEL))


def _in_proj_b_kernel(x_ref, w_ref, q_ref, k_ref, vt_ref, qm_ref):
    assert WIN_KV_WIDTH == PROJ_CHUNK and MEM_WIDTH == PROJ_CHUNK
    xb = x_ref[...].astype(BF16)
    n_q = TOK_WIDTH // PROJ_CHUNK
    for c in range(B_IN // PROJ_CHUNK):
        cols = slice(c * PROJ_CHUNK, (c + 1) * PROJ_CHUNK)
        r = jnp.dot(xb, w_ref[:, cols], preferred_element_type=F32)
        if c < n_q:
            q_ref[:, cols] = (r * (QK_SCALE * LOG2E)).astype(BF16)
        elif c == n_q:
            k_ref[...] = r.astype(BF16)
        elif c == n_q + 1:
            rt = r.T.astype(BF16)
            for t in range(PROJ_TM // WINDOW):
                vt_ref[t] = rt[:, t * WINDOW:(t + 1) * WINDOW]
        else:
            qm_ref[...] = (r * QK_SCALE).astype(BF16)


def _in_proj_b(x, w_in, batch, seq):
    n = x.shape[0]
    tiles_per_batch = seq // PROJ_TM
    est = D_MODEL * B_IN * 2 + 2 * PROJ_TM * D_MODEL * 4 + 2 * PROJ_TM * B_IN * 2 + 4 * PROJ_TM * PROJ_CHUNK * 4
    return pl.pallas_call(
        _in_proj_b_kernel,
        out_shape=(
            jax.ShapeDtypeStruct((n, TOK_WIDTH), BF16),
            jax.ShapeDtypeStruct((n, WIN_KV_WIDTH), BF16),
            jax.ShapeDtypeStruct((batch, seq // WINDOW, WIN_KV_WIDTH, WINDOW), BF16),
            jax.ShapeDtypeStruct((n, MEM_WIDTH), BF16),
        ),
        grid=(n // PROJ_TM,),
        in_specs=[
            pl.BlockSpec((PROJ_TM, D_MODEL), lambda i: (i, 0)),
            _resident((D_MODEL, B_IN), lambda i: (0, 0)),
        ],
        out_specs=(
            pl.BlockSpec((PROJ_TM, TOK_WIDTH), lambda i: (i, 0)),
            pl.BlockSpec((PROJ_TM, WIN_KV_WIDTH), lambda i: (i, 0)),
            pl.BlockSpec((None, PROJ_TM // WINDOW, WIN_KV_WIDTH, WINDOW),
                         lambda i: (i // tiles_per_batch, i % tiles_per_batch, 0, 0)),
            pl.BlockSpec((PROJ_TM, MEM_WIDTH), lambda i: (i, 0)),
        ),
        compiler_params=pltpu.CompilerParams(
            dimension_semantics=("arbitrary",), vmem_limit_bytes=_vmem_limit(est)),
        name="in_proj_b",
    )(x, w_in.astype(BF16))


def _win_attn_kernel(slopes_ref, sink_ref, q_ref, k_ref, vt_ref, o_ref,
                     rel_ref, w_ref, st_ref, max_ref, p_ref, linv_ref, ot_ref, *, seq):
    i = pl.program_id(1)
    span = 3 * WINDOW
    nblk = WIN_TQ // WINDOW
    pair = 2 * HEAD_DIM
    assert WINDOW == V7X_LANES and nblk >= 4

    rel_ref[...] = (lax.broadcasted_iota(jnp.int32, (span, WINDOW), 0)
                    - lax.broadcasted_iota(jnp.int32, (span, WINDOW), 1)).astype(F32)

    def rows_of(n):
        return pl.ds(n * WINDOW, WINDOW) if isinstance(n, int) else pl.ds(pl.multiple_of(n * WINDOW, WINDOW), WINDOW)

    def window_of(n):
        gn = i * nblk + n
        return jnp.clip(gn - 1, 0, seq // WINDOW - 3), gn

    def prep_q(n):
        qt = q_ref[rows_of(n), :].astype(F32).T
        zeros = jnp.zeros((HEAD_DIM, WINDOW), F32)
        for g in range(WIN_KV_HEADS):
            cols = []
            for hh in range(WIN_GROUP):
                x = qt[(g * WIN_GROUP + hh) * HEAD_DIM:(g * WIN_GROUP + hh + 1) * HEAD_DIM, :]
                cols.append(jnp.concatenate([x, zeros] if g % 2 == 0 else [zeros, x], axis=0))
            w_ref[g] = jnp.concatenate(cols, axis=1).astype(BF16)

    def scores(n, g):
        c0, gn = window_of(n)
        keys = k_ref[pl.ds(pl.multiple_of(c0 * WINDOW, WINDOW), span), (g // 2) * pair:(g // 2 + 1) * pair]
        st = jnp.dot(keys, w_ref[g], preferred_element_type=F32)
        dist = jnp.abs(((gn - c0) * WINDOW).astype(F32) - rel_ref[...])
        penalty = jnp.where(dist <= WINDOW, 0.0, -NEG)
        for hh in range(WIN_GROUP):
            head = g * WIN_GROUP + hh
            lanes = slice(hh * WINDOW, (hh + 1) * WINDOW)
            s = st[:, lanes] - (slopes_ref[head] * dist + penalty)
            st_ref[g, :, lanes] = s
            max_ref[g, :, lanes] = jnp.maximum(jnp.max(s, axis=0, keepdims=True), sink_ref[head])

    def softmax(n, g):
        for hh in range(WIN_GROUP):
            head = g * WIN_GROUP + hh
            lanes = slice(hh * WINDOW, (hh + 1) * WINDOW)
            m = max_ref[g, :, lanes]
            p = jnp.exp2(st_ref[g, :, lanes] - m)
            p_ref[g, :, lanes] = p.astype(BF16)
            linv_ref[g, :, lanes] = 1.0 / (jnp.sum(p, axis=0, keepdims=True) + jnp.exp2(sink_ref[head] - m))

    def pv(n, g):
        c0, _ = window_of(n)
        vt = jnp.concatenate([vt_ref[c0 + e, g * HEAD_DIM:(g + 1) * HEAD_DIM, :] for e in range(3)], axis=1)
        ot_ref[g] = jnp.dot(vt, p_ref[g], preferred_element_type=F32) * linv_ref[g]

    def assemble(n):
        for pp in range(WIN_Q_HEADS // 2):
            parts = []
            for head in (2 * pp, 2 * pp + 1):
                g, hh = divmod(head, WIN_GROUP)
                parts.append(ot_ref[g, :, hh * WINDOW:(hh + 1) * WINDOW])
            o_ref[rows_of(n), pp * pair:(pp + 1) * pair] = jnp.concatenate(parts, axis=0).T.astype(BF16)

    def iteration(n, do_assemble, do_pv, do_softmax, do_scores):
        nxt = min(n + 1, nblk - 1) if isinstance(n, int) else jnp.minimum(n + 1, nblk - 1)
        if do_assemble:
            assemble(n - 2)
        if do_scores:
            prep_q(nxt)
        for g in range(WIN_KV_HEADS):
            if do_pv:
                pv(n - 1, g)
            if do_softmax:
                softmax(n, g)
            if do_scores:
                scores(nxt, g)

    prep_q(0)
    for g in range(WIN_KV_HEADS):
        scores(0, g)
    iteration(0, False, False, True, True)
    iteration(1, False, True, True, True)

    def body(n, carry):
        iteration(n, True, True, True, True)
        return carry

    lax.fori_loop(2, nblk, body, 0)
    iteration(nblk, True, True, False, False)
    iteration(nblk + 1, True, False, False, False)


def _win_attn(q, k, vt, sink, batch, seq):
    nq = seq // WIN_TQ
    span = 3 * WINDOW
    slopes = _alibi_slopes(WIN_Q_HEADS) * LOG2E
    est = (2 * 2 * seq * WIN_KV_WIDTH * 2 + 4 * WIN_TQ * TOK_WIDTH * 2
           + WIN_KV_HEADS * span * span * (4 + 2) + 8 * span * span * 4)
    return pl.pallas_call(
        functools.partial(_win_attn_kernel, seq=seq),
        out_shape=jax.ShapeDtypeStruct((batch * seq, TOK_WIDTH), BF16),
        grid=(batch, nq),
        in_specs=[
            pl.BlockSpec(memory_space=pltpu.SMEM),
            pl.BlockSpec(memory_space=pltpu.SMEM),
            pl.BlockSpec((WIN_TQ, TOK_WIDTH), lambda b, i: (b * nq + i, 0)),
            pl.BlockSpec((seq, WIN_KV_WIDTH), lambda b, i: (b, 0)),
            pl.BlockSpec((None, seq // WINDOW, WIN_KV_WIDTH, WINDOW), lambda b, i: (b, 0, 0, 0)),
        ],
        out_specs=pl.BlockSpec((WIN_TQ, TOK_WIDTH), lambda b, i: (b * nq + i, 0)),
        scratch_shapes=[
            pltpu.VMEM((span, WINDOW), F32),
            pltpu.VMEM((WIN_KV_HEADS, 2 * HEAD_DIM, span), BF16),
            pltpu.VMEM((WIN_KV_HEADS, span, span), F32),
            pltpu.VMEM((WIN_KV_HEADS, 1, span), F32),
            pltpu.VMEM((WIN_KV_HEADS, span, span), BF16),
            pltpu.VMEM((WIN_KV_HEADS, 1, span), F32),
            pltpu.VMEM((WIN_KV_HEADS, HEAD_DIM, span), F32),
        ],
        compiler_params=pltpu.CompilerParams(
            dimension_semantics=("arbitrary", "arbitrary"), vmem_limit_bytes=_vmem_limit(est)),
        name="win_attn",
    )(slopes, sink.astype(F32) * LOG2E, q, k, vt)


def kernel(x, mem, ffn1_w13, ffn1_w2, ln1_g, ln1_b, w_mem_kv, w_out, ln2_g, ln2_b, ffn2_w13, ffn2_w2,
           ln3_g, ln3_b, a_w_in, a_lambda_q1, a_lambda_k1, a_lambda_q2, a_lambda_k2, a_subln_g, b_w_in, b_sink):
    batch, seq, _ = x.shape
    xf = x.reshape(batch * seq, D_MODEL)
    w_out = w_out.astype(BF16)
    for i in range(DEPTH):
        xf = _ffn_ln(xf, ffn1_w13, ffn1_w2, i, ln1_g[i], ln1_b[i])
        j = i // N_MIXERS
        kt, v = _mem_kv(mem, w_mem_kv[i])
        if i % N_MIXERS == 0:
            lam_init = 0.8 - 0.6 * math.exp(-0.3 * i)
            q, k, vt, qm = _in_proj_a(xf, a_w_in[j], batch, seq)
            o = _diff_attn(q, k, vt, a_lambda_q1[j], a_lambda_k1[j], a_lambda_q2[j], a_lambda_k2[j],
                           a_subln_g[j], lam_init, batch, seq)
        else:
            q, k, vt, qm = _in_proj_b(xf, b_w_in[j], batch, seq)
            o = _win_attn(q, k, vt, b_sink[j], batch, seq)
        xf = _mix_out(xf, o, qm, kt, v, w_out, i, ln2_g[i], ln2_b[i], seq)
        xf = _ffn_ln(xf, ffn2_w13, ffn2_w2, i, ln3_g[i], ln3_b[i])
    return xf.reshape(batch, seq, D_MODEL)
```

```python
import functools
import math

import numpy as np
import jax
import jax.numpy as jnp
from jax import lax
from jax.experimental import pallas as pl
from jax.experimental.pallas import tpu as pltpu

D_MODEL = 1024
DEPTH = 2
N_MIXERS = 2
N_MEM = 256
HEAD_DIM = 64
MEM_HEADS = 4
MEM_WIDTH = MEM_HEADS * HEAD_DIM
TOK_WIDTH = D_MODEL - MEM_WIDTH
DIFF_HEADS = TOK_WIDTH // (2 * HEAD_DIM)
DIFF_V_DIM = 2 * HEAD_DIM
QK_A = DIFF_HEADS * 2 * HEAD_DIM
WIN_Q_HEADS = TOK_WIDTH // HEAD_DIM
WIN_GROUP = 3
WIN_KV_HEADS = WIN_Q_HEADS // WIN_GROUP
WIN_KV_WIDTH = WIN_KV_HEADS * HEAD_DIM
WINDOW = 128
D_FF = 2816
A_IN = 2 * QK_A + TOK_WIDTH + MEM_WIDTH
B_IN = TOK_WIDTH + 2 * WIN_KV_WIDTH + MEM_WIDTH
N_A = (DEPTH + 1) // 2
DN_ALPHA = (2 * DEPTH) ** 0.25
LN_EPS = 1e-5
NEG = -1e30
QK_SCALE = HEAD_DIM ** -0.5
LOG2E = math.log2(math.e)

F32 = jnp.float32
BF16 = jnp.bfloat16

V7X_LANES = 128
V7X_MXU_DIM = 256
V7X_VMEM_BYTES = 64 * 1024 * 1024

FFN_TM = 512
FF_CHUNK = V7X_MXU_DIM
PROJ_TM = 1024
PROJ_CHUNK = V7X_MXU_DIM
DIFF_TQ = 256
DIFF_TK = 256
DIFF_TILES = 4
DIFF_DEPTH = 4
MIX_TM = 1024
WIN_TQ = 1024


def _vmem_limit(nbytes):
    return int(min(V7X_VMEM_BYTES - 8 * 1024 * 1024, nbytes * 3 // 2))


def _resident(block_shape, index_map):
    return pl.BlockSpec(block_shape, index_map, pipeline_mode=pl.Buffered(1))


def _layer_norm(y, g, b):
    mu = jnp.mean(y, axis=-1, keepdims=True)
    yc = y - mu
    var = jnp.mean(yc * yc, axis=-1, keepdims=True)
    return yc * lax.rsqrt(var + LN_EPS) * g + b


def _alibi_slopes(n):
    return jnp.asarray(2.0 ** (-8.0 * np.arange(1, n + 1) / n), dtype=F32)


def _ffn_ln_kernel(x_ref, wg_ref, wu_ref, w2_ref, g_ref, b_ref, o_ref, acc_ref):
    x = x_ref[...]
    xb = x.astype(BF16)
    for c in range(D_FF // FF_CHUNK):
        cols = slice(c * FF_CHUNK, (c + 1) * FF_CHUNK)
        gate = jnp.dot(xb, wg_ref[:, cols].astype(BF16), preferred_element_type=F32)
        up = jnp.dot(xb, wu_ref[:, cols].astype(BF16), preferred_element_type=F32)
        act = (gate / (1.0 + jnp.exp(-gate)) * up).astype(BF16)
        part = jnp.dot(act, w2_ref[cols, :].astype(BF16), preferred_element_type=F32)
        if c == 0:
            acc_ref[...] = part
        else:
            acc_ref[...] += part
    y = DN_ALPHA * x + 0.5 * acc_ref[...]
    o_ref[...] = _layer_norm(y, g_ref[...], b_ref[...])


def _ffn_ln(x, w13, w2, layer, g, b):
    n = x.shape[0]
    est = (2 * D_MODEL * D_FF + D_FF * D_MODEL) * 4 + 5 * FFN_TM * D_MODEL * 4 + 4 * FFN_TM * FF_CHUNK * 4
    return pl.pallas_call(
        _ffn_ln_kernel,
        out_shape=jax.ShapeDtypeStruct((n, D_MODEL), F32),
        grid=(n // FFN_TM,),
        in_specs=[
            pl.BlockSpec((FFN_TM, D_MODEL), lambda i: (i, 0)),
            _resident((None, D_MODEL, D_FF), lambda i: (layer, 0, 0)),
            _resident((None, D_MODEL, D_FF), lambda i: (layer, 0, 1)),
            _resident((None, D_FF, D_MODEL), lambda i: (layer, 0, 0)),
            _resident((1, D_MODEL), lambda i: (0, 0)),
            _resident((1, D_MODEL), lambda i: (0, 0)),
        ],
        out_specs=pl.BlockSpec((FFN_TM, D_MODEL), lambda i: (i, 0)),
        scratch_shapes=[pltpu.VMEM((FFN_TM, D_MODEL), F32)],
        compiler_params=pltpu.CompilerParams(
            dimension_semantics=("arbitrary",), vmem_limit_bytes=_vmem_limit(est)),
        name="ffn_ln",
    )(x, w13, w13, w2, g.reshape(1, D_MODEL), b.reshape(1, D_MODEL))


def _in_proj_a_kernel(x_ref, w_ref, q_ref, k_ref, vt_ref, qm_ref):
    xb = x_ref[...].astype(BF16)
    heads_per_chunk = PROJ_CHUNK // DIFF_V_DIM
    n_q = QK_A // PROJ_CHUNK
    n_qk = 2 * n_q
    n_v = TOK_WIDTH // PROJ_CHUNK
    for c in range(A_IN // PROJ_CHUNK):
        r = jnp.dot(xb, w_ref[:, c * PROJ_CHUNK:(c + 1) * PROJ_CHUNK], preferred_element_type=F32)
        if c < n_qk:
            dst, c0, scale = (q_ref, 0, QK_SCALE * LOG2E) if c < n_q else (k_ref, n_q, 1.0)
            for hh in range(heads_per_chunk):
                dst[(c - c0) * heads_per_chunk + hh] = (
                    r[:, hh * DIFF_V_DIM:(hh + 1) * DIFF_V_DIM] * scale).astype(BF16)
        elif c < n_qk + n_v:
            rt = r.T.astype(BF16)
            for hh in range(heads_per_chunk):
                for t in range(PROJ_TM // DIFF_TK):
                    vt_ref[(c - n_qk) * heads_per_chunk + hh, t] = rt[
                        hh * DIFF_V_DIM:(hh + 1) * DIFF_V_DIM, t * DIFF_TK:(t + 1) * DIFF_TK]
        else:
            qm_ref[...] = (r * QK_SCALE).astype(BF16)


def _in_proj_a(x, w_in, batch, seq):
    n = x.shape[0]
    w_in = w_in.astype(BF16)
    tiles_per_batch = seq // PROJ_TM
    chunks_per_tile = PROJ_TM // DIFF_TK
    est = D_MODEL * A_IN * 2 + 2 * PROJ_TM * D_MODEL * 4 + 2 * PROJ_TM * A_IN * 2 + 4 * PROJ_TM * PROJ_CHUNK * 4
    head_major = jax.ShapeDtypeStruct((batch, DIFF_HEADS, seq, DIFF_V_DIM), BF16)
    head_major_spec = pl.BlockSpec((None, DIFF_HEADS, PROJ_TM, DIFF_V_DIM),
                                   lambda i: (i // tiles_per_batch, 0, i % tiles_per_batch, 0))
    return pl.pallas_call(
        _in_proj_a_kernel,
        out_shape=(
            head_major,
            head_major,
            jax.ShapeDtypeStruct((batch, DIFF_HEADS, seq // DIFF_TK, DIFF_V_DIM, DIFF_TK), BF16),
            jax.ShapeDtypeStruct((n, MEM_WIDTH), BF16),
        ),
        grid=(n // PROJ_TM,),
        in_specs=[
            pl.BlockSpec((PROJ_TM, D_MODEL), lambda i: (i, 0)),
            _resident((D_MODEL, A_IN), lambda i: (0, 0)),
        ],
        out_specs=(
            head_major_spec,
            head_major_spec,
            pl.BlockSpec((None, DIFF_HEADS, chunks_per_tile, DIFF_V_DIM, DIFF_TK),
                         lambda i: (i // tiles_per_batch, 0, i % tiles_per_batch, 0, 0)),
            pl.BlockSpec((PROJ_TM, MEM_WIDTH), lambda i: (i, 0)),
        ),
        compiler_params=pltpu.CompilerParams(
            dimension_semantics=("arbitrary",), vmem_limit_bytes=_vmem_limit(est)),
        name="in_proj_a",
    )(x, w_in)


def _diff_attn_kernel(slopes_ref, q_ref, k_ref, vt_ref, lq1_ref, lk1_ref, lq2_ref, lk2_ref, g_ref,
                      o_ref, qt_ref, pos_ref, bias_ref, sr_ref, m_ref, l_ref, acc_ref,
                      st_ref, shift_ref, max_ref, p_ref, alpha_ref, *, lam_init, seq):
    h = pl.program_id(1)
    tq, tk = DIFF_TQ, DIFF_TK
    assert tq == tk and tk <= 256
    n_steps = seq // tk
    n_strips = 2 * tq // V7X_LANES
    slope = slopes_ref[h]
    tiles = [pl.program_id(2) * DIFF_TILES + u for u in range(DIFF_TILES)]

    slope_v = jnp.full((16, 2 * tq), slope, F32)
    s_hi = slope_v.astype(BF16).astype(F32)
    s_mid = (slope_v - s_hi).astype(BF16).astype(F32)
    s_lo = slope_v - s_hi - s_mid
    prow = lax.broadcasted_iota(jnp.int32, (16, 2 * tq), 0)
    pieces = jnp.where(prow == 0, s_hi, jnp.where(prow == 1, s_mid, jnp.where(prow == 2, s_lo, 0.0)))
    for u in range(DIFF_TILES):
        qt = q_ref[u * tq:(u + 1) * tq, :].astype(F32).T
        row = lax.broadcasted_iota(jnp.int32, qt.shape, 0)
        zero = jnp.zeros_like(qt)
        qt_ref[u, 0:DIFF_V_DIM, :tq] = jnp.where(row < HEAD_DIM, qt, zero).astype(BF16)
        qt_ref[u, 0:DIFF_V_DIM, tq:] = jnp.where(row >= HEAD_DIM, qt, zero).astype(BF16)
        qt_ref[u, DIFF_V_DIM:DIFF_V_DIM + 16, :] = pieces.astype(BF16)
        qt_ref[u, DIFF_V_DIM + 16:, :] = jnp.zeros((DIFF_V_DIM - 16, 2 * tq), BF16)

    r_s = lax.broadcasted_iota(jnp.int32, (tk, V7X_LANES), 0).astype(F32)
    lane = lax.broadcasted_iota(jnp.int32, (tk, V7X_LANES), 1)
    r_s = jnp.where(lane < 3, r_s, 0.0)
    pos_ref[0] = r_s.astype(BF16)
    pos_ref[1] = (-r_s).astype(BF16)
    pos_ref[2] = jnp.zeros((tk, V7X_LANES), BF16)

    rel = (lax.broadcasted_iota(jnp.int32, (tk, tq), 0) - lax.broadcasted_iota(jnp.int32, (tk, tq), 1))
    bias_ref[...] = slope * jnp.abs(rel).astype(F32)
    r_t = lax.broadcasted_iota(jnp.int32, (1, 2 * tq), 1)
    sr_ref[...] = slope * jnp.where(r_t >= tq, r_t - tq, r_t).astype(F32)

    m_ref[...] = jnp.full(m_ref.shape, -jnp.inf, F32)
    l_ref[...] = jnp.zeros(l_ref.shape, F32)
    acc_ref[...] = jnp.zeros(acc_ref.shape, F32)

    def keys(j):
        return k_ref[pl.ds(pl.multiple_of(j * tk, tk), tk), :]

    def slot_tiles(s, u):
        i = tiles[u]
        other = lambda x: jnp.where(x >= i, x + 1, x)
        first = i if (isinstance(s, int) and s == 0) else jnp.where(s == 0, i, other(2 * s - 1))
        return first, other(2 * s)

    def strip(c):
        return slice(c * V7X_LANES, (c + 1) * V7X_LANES)

    def scores(s, b, u):
        i = tiles[u]
        halves = []
        for e, j in enumerate(slot_tiles(s, u)):
            if isinstance(s, int) and s == 0 and e == 0:
                halves.append(jnp.concatenate([keys(j), pos_ref[2]], axis=1))
                shift_ref[b, e] = jnp.zeros((1, 2 * tq), F32)
            else:
                after = (j > i).astype(jnp.int32)
                halves.append(jnp.concatenate([keys(j), pos_ref[after]], axis=1))
                sign = (1 - 2 * after).astype(F32)
                tile_dist = (jnp.abs(j - i) * tk).astype(F32)
                shift_ref[b, e] = -(sign * sr_ref[...]) - slope * tile_dist
        st = jnp.dot(jnp.concatenate(halves, axis=0), qt_ref[u], preferred_element_type=F32)
        for e in range(2):
            for c in range(n_strips):
                blk = st[e * tk:(e + 1) * tk, strip(c)]
                if isinstance(s, int) and s == 0 and e == 0:
                    blk = blk - bias_ref[:, strip(c % (tq // V7X_LANES))]
                st_ref[b, e, c] = blk
                max_ref[b, e, :, strip(c)] = jnp.max(blk, axis=0, keepdims=True)

    def softmax(b, u):
        shift0, shift1 = shift_ref[b, 0], shift_ref[b, 1]
        m_old = m_ref[u]
        m_new = jnp.maximum(m_old, jnp.maximum(max_ref[b, 0] + shift0, max_ref[b, 1] + shift1))
        alpha = jnp.exp2(m_old - m_new)
        m_ref[u] = m_new
        alpha_ref[b] = alpha
        u0, u1 = m_new - shift0, m_new - shift1
        sums = []
        for c in range(n_strips):
            lanes = strip(c)
            p0 = jnp.exp2(st_ref[b, 0, c] - u0[:, lanes])
            p1 = jnp.exp2(st_ref[b, 1, c] - u1[:, lanes])
            sums.append(jnp.sum(p0, axis=0, keepdims=True) + jnp.sum(p1, axis=0, keepdims=True))
            p_ref[b, 0, c] = p0.astype(BF16)
            p_ref[b, 1, c] = p1.astype(BF16)
        l_ref[u] = alpha * l_ref[u] + jnp.concatenate(sums, axis=1)

    def pv(s, b, u):
        j0, j1 = slot_tiles(s, u)
        p = jnp.concatenate([jnp.concatenate([p_ref[b, e, c] for c in range(n_strips)], axis=1)
                             for e in range(2)], axis=0)
        v = jnp.concatenate([vt_ref[j0], vt_ref[j1]], axis=1)
        acc_ref[u] = alpha_ref[b] * acc_ref[u] + jnp.dot(v, p, preferred_element_type=F32)

    lam = (jnp.exp(jnp.sum(lq1_ref[...] * lk1_ref[...], axis=-1, keepdims=True))
           - jnp.exp(jnp.sum(lq2_ref[...] * lk2_ref[...], axis=-1, keepdims=True)) + lam_init)

    def finalize(u):
        o_t = acc_ref[u] / l_ref[u]
        a = (o_t[:, :tq] - lam * o_t[:, tq:]).T
        a = a * lax.rsqrt(jnp.mean(a * a, axis=-1, keepdims=True) + LN_EPS) * g_ref[...]
        o_ref[u * tq:(u + 1) * tq, :] = (a * (1.0 - lam_init)).astype(BF16)

    depth = DIFF_DEPTH
    n_slots = n_steps // 2
    assert n_steps % 2 == 0 and (n_slots - 2 * depth) % depth == 0

    def stage(s, r, u):
        if not (isinstance(s, int) and s < depth):
            pv(s - depth, r, u)
        elif u > 0:
            pv(n_slots - depth + s, r, u - 1)
        softmax(r, u)
        if not (isinstance(s, int) and s + depth >= n_slots):
            scores(s + depth, r, u)
        elif u + 1 < DIFF_TILES:
            scores(s + depth - n_slots, r, u + 1)

    for s in range(depth):
        scores(s, s, 0)
    for u in range(DIFF_TILES):
        for s in range(depth):
            stage(s, s, u)
        if u > 0:
            finalize(u - 1)

        def body(k, carry, u=u):
            for r in range(depth):
                stage(depth * (k + 1) + r, r, u)
            return carry

        lax.fori_loop(0, (n_slots - 2 * depth) // depth, body, 0)
        for s in range(n_slots - depth, n_slots):
            stage(s, s % depth, u)
    for s in range(n_slots - depth, n_slots):
        pv(s, s % depth, DIFF_TILES - 1)
    finalize(DIFF_TILES - 1)


def _diff_attn(q, k, vt, lq1, lk1, lq2, lk2, subln_g, lam_init, batch, seq):
    nq = seq // (DIFF_TILES * DIFF_TQ)
    rows = DIFF_TILES * DIFF_TQ
    strips = (2 * DIFF_TQ // V7X_LANES, DIFF_TK, V7X_LANES)
    slopes = _alibi_slopes(DIFF_HEADS) * LOG2E
    vec = lambda v: v.reshape(1, -1).astype(F32)
    est = (2 * 2 * seq * DIFF_V_DIM * 2 + 4 * rows * DIFF_V_DIM * 2
           + DIFF_TILES * 3 * DIFF_V_DIM * 2 * DIFF_TQ * 4
           + DIFF_DEPTH * 2 * DIFF_TK * 2 * DIFF_TQ * (4 + 2)
           + 4 * DIFF_TK * 2 * DIFF_TQ * 4 + 2 * DIFF_V_DIM * 2 * DIFF_TQ * 4)
    small = lambda n: pl.BlockSpec((1, n), lambda b, h, i: (0, 0))
    return pl.pallas_call(
        functools.partial(_diff_attn_kernel, lam_init=lam_init, seq=seq),
        out_shape=jax.ShapeDtypeStruct((batch * seq, TOK_WIDTH), BF16),
        grid=(batch, DIFF_HEADS, nq),
        in_specs=[
            pl.BlockSpec(memory_space=pltpu.SMEM),
            pl.BlockSpec((None, None, rows, DIFF_V_DIM), lambda b, h, i: (b, h, i, 0)),
            pl.BlockSpec((None, None, seq, DIFF_V_DIM), lambda b, h, i: (b, h, 0, 0)),
            pl.BlockSpec((None, None, seq // DIFF_TK, DIFF_V_DIM, DIFF_TK), lambda b, h, i: (b, h, 0, 0, 0)),
            small(HEAD_DIM), small(HEAD_DIM), small(HEAD_DIM), small(HEAD_DIM), small(DIFF_V_DIM),
        ],
        out_specs=pl.BlockSpec((rows, DIFF_V_DIM), lambda b, h, i: (b * nq + i, h)),
        scratch_shapes=[
            pltpu.VMEM((DIFF_TILES, 2 * DIFF_V_DIM, 2 * DIFF_TQ), BF16),
            pltpu.VMEM((3, DIFF_TK, V7X_LANES), BF16),
            pltpu.VMEM((DIFF_TK, DIFF_TQ), F32),
            pltpu.VMEM((1, 2 * DIFF_TQ), F32),
            pltpu.VMEM((DIFF_TILES, 1, 2 * DIFF_TQ), F32),
            pltpu.VMEM((DIFF_TILES, 1, 2 * DIFF_TQ), F32),
            pltpu.VMEM((DIFF_TILES, DIFF_V_DIM, 2 * DIFF_TQ), F32),
            pltpu.VMEM((DIFF_DEPTH, 2) + strips, F32),
            pltpu.VMEM((DIFF_DEPTH, 2, 1, 2 * DIFF_TQ), F32),
            pltpu.VMEM((DIFF_DEPTH, 2, 1, 2 * DIFF_TQ), F32),
            pltpu.VMEM((DIFF_DEPTH, 2) + strips, BF16),
            pltpu.VMEM((DIFF_DEPTH, 1, 2 * DIFF_TQ), F32),
        ],
        compiler_params=pltpu.CompilerParams(
            dimension_semantics=("arbitrary", "arbitrary", "arbitrary"), vmem_limit_bytes=_vmem_limit(est)),
        name="diff_attn",
    )(slopes, q, k, vt, vec(lq1), vec(lk1), vec(lq2), vec(lk2), vec(subln_g))


def _mem_kv_kernel(mem_ref, w_ref, kt_ref, v_ref):
    kv = jnp.dot(mem_ref[...].astype(BF16), w_ref[...], preferred_element_type=F32)
    kt = kv[:, :MEM_WIDTH].T
    v = kv[:, MEM_WIDTH:]
    pair_rows = 2 * HEAD_DIM
    row = lax.broadcasted_iota(jnp.int32, (pair_rows, N_MEM), 0)
    col = lax.broadcasted_iota(jnp.int32, (N_MEM, MEM_WIDTH), 1)
    for h in range(MEM_HEADS):
        pair = kt[(h // 2) * pair_rows:(h // 2 + 1) * pair_rows, :]
        keep = (row >= (h % 2) * HEAD_DIM) & (row < (h % 2 + 1) * HEAD_DIM)
        kt_ref[h] = jnp.where(keep, pair, jnp.zeros_like(pair)).astype(BF16)
        keep_v = (col >= h * HEAD_DIM) & (col < (h + 1) * HEAD_DIM)
        v_ref[h] = jnp.where(keep_v, v, jnp.zeros_like(v)).astype(BF16)


def _mem_kv(mem, w_mem_kv):
    batch = mem.shape[0]
    return pl.pallas_call(
        _mem_kv_kernel,
        out_shape=(
            jax.ShapeDtypeStruct((batch, MEM_HEADS, 2 * HEAD_DIM, N_MEM), BF16),
            jax.ShapeDtypeStruct((batch, MEM_HEADS, N_MEM, MEM_WIDTH), BF16),
        ),
        grid=(batch,),
        in_specs=[
            pl.BlockSpec((None, N_MEM, D_MODEL), lambda b: (b, 0, 0)),
            pl.BlockSpec((D_MODEL, 2 * MEM_WIDTH), lambda b: (0, 0)),
        ],
        out_specs=(
            pl.BlockSpec((None, MEM_HEADS, 2 * HEAD_DIM, N_MEM), lambda b: (b, 0, 0, 0)),
            pl.BlockSpec((None, MEM_HEADS, N_MEM, MEM_WIDTH), lambda b: (b, 0, 0, 0)),
        ),
        name="mem_kv",
    )(mem, w_mem_kv.astype(BF16))


def _mix_out_kernel(x_ref, o_ref, qm_ref, kt_ref, v_ref, wo_ref, wm_ref, g_ref, b_ref, out_ref):
    qm = qm_ref[...]
    pair_w = 2 * HEAD_DIM
    scores = [jnp.dot(qm[:, (h // 2) * pair_w:(h // 2 + 1) * pair_w], kt_ref[h],
                      preferred_element_type=F32) for h in range(MEM_HEADS)]
    exps = [jnp.exp(s - jnp.max(s, axis=-1, keepdims=True)) for s in scores]
    probs = [(e / jnp.sum(e, axis=-1, keepdims=True)).astype(BF16) for e in exps]
    mem_o = None
    for h in range(MEM_HEADS):
        part = jnp.dot(probs[h], v_ref[h], preferred_element_type=F32)
        mem_o = part if mem_o is None else mem_o + part
    y = jnp.dot(o_ref[...], wo_ref[...], preferred_element_type=F32)
    y = y + jnp.dot(mem_o.astype(BF16), wm_ref[...], preferred_element_type=F32)
    out_ref[...] = _layer_norm(DN_ALPHA * x_ref[...] + y, g_ref[...], b_ref[...])


def _mix_out(x, o, qm, kt, v, w_out, layer, g, b, seq):
    n = x.shape[0]
    tiles_per_batch = seq // MIX_TM
    est = (D_MODEL * D_MODEL * 2 + 4 * MIX_TM * D_MODEL * 4 + 2 * MIX_TM * D_MODEL * 2
           + 4 * MEM_HEADS * N_MEM * MEM_WIDTH * 2 + 6 * MIX_TM * D_MODEL * 4)
    return pl.pallas_call(
        _mix_out_kernel,
        out_shape=jax.ShapeDtypeStruct((n, D_MODEL), F32),
        grid=(n // MIX_TM,),
        in_specs=[
            pl.BlockSpec((MIX_TM, D_MODEL), lambda i: (i, 0)),
            pl.BlockSpec((MIX_TM, TOK_WIDTH), lambda i: (i, 0)),
            pl.BlockSpec((MIX_TM, MEM_WIDTH), lambda i: (i, 0)),
            pl.BlockSpec((None, MEM_HEADS, 2 * HEAD_DIM, N_MEM), lambda i: (i // tiles_per_batch, 0, 0, 0)),
            pl.BlockSpec((None, MEM_HEADS, N_MEM, MEM_WIDTH), lambda i: (i // tiles_per_batch, 0, 0, 0)),
            _resident((None, TOK_WIDTH, D_MODEL), lambda i: (layer, 0, 0)),
            _resident((None, MEM_WIDTH, D_MODEL), lambda i: (layer, TOK_WIDTH // MEM_WIDTH, 0)),
            _resident((1, D_MODEL), lambda i: (0, 0)),
            _resident((1, D_MODEL), lambda i: (0, 0)),
        ],
        out_specs=pl.BlockSpec((MIX_TM, D_MODEL), lambda i: (i, 0)),
        compiler_params=pltpu.CompilerParams(
            dimension_semantics=("arbitrary",), vmem_limit_bytes=_vmem_limit(est)),
        name="mix_out",
    )(x, o, qm, kt, v, w_out, w_out, g.reshape(1, D_MODEL), b.reshape(1, D_MODEL))


def _in_proj_b_kernel(x_ref, w_ref, q_ref, k_ref, vt_ref, qm_ref):
    assert WIN_KV_WIDTH == PROJ_CHUNK and MEM_WIDTH == PROJ_CHUNK
    xb = x_ref[...].astype(BF16)
    n_q = TOK_WIDTH // PROJ_CHUNK
    for c in range(B_IN // PROJ_CHUNK):
        cols = slice(c * PROJ_CHUNK, (c + 1) * PROJ_CHUNK)
        r = jnp.dot(xb, w_ref[:, cols], preferred_element_type=F32)
        if c < n_q:
            q_ref[:, cols] = (r * (QK_SCALE * LOG2E)).astype(BF16)
        elif c == n_q:
            k_ref[...] = r.astype(BF16)
        elif c == n_q + 1:
            rt = r.T.astype(BF16)
            for t in range(PROJ_TM // WINDOW):
                vt_ref[t] = rt[:, t * WINDOW:(t + 1) * WINDOW]
        else:
            qm_ref[...] = (r * QK_SCALE).astype(BF16)


def _in_proj_b(x, w_in, batch, seq):
    n = x.shape[0]
    tiles_per_batch = seq // PROJ_TM
    est = D_MODEL * B_IN * 2 + 2 * PROJ_TM * D_MODEL * 4 + 2 * PROJ_TM * B_IN * 2 + 4 * PROJ_TM * PROJ_CHUNK * 4
    return pl.pallas_call(
        _in_proj_b_kernel,
        out_shape=(
            jax.ShapeDtypeStruct((n, TOK_WIDTH), BF16),
            jax.ShapeDtypeStruct((n, WIN_KV_WIDTH), BF16),
            jax.ShapeDtypeStruct((batch, seq // WINDOW, WIN_KV_WIDTH, WINDOW), BF16),
            jax.ShapeDtypeStruct((n, MEM_WIDTH), BF16),
        ),
        grid=(n // PROJ_TM,),
        in_specs=[
            pl.BlockSpec((PROJ_TM, D_MODEL), lambda i: (i, 0)),
            _resident((D_MODEL, B_IN), lambda i: (0, 0)),
        ],
        out_specs=(
            pl.BlockSpec((PROJ_TM, TOK_WIDTH), lambda i: (i, 0)),
            pl.BlockSpec((PROJ_TM, WIN_KV_WIDTH), lambda i: (i, 0)),
            pl.BlockSpec((None, PROJ_TM // WINDOW, WIN_KV_WIDTH, WINDOW),
                         lambda i: (i // tiles_per_batch, i % tiles_per_batch, 0, 0)),
            pl.BlockSpec((PROJ_TM, MEM_WIDTH), lambda i: (i, 0)),
        ),
        compiler_params=pltpu.CompilerParams(
            dimension_semantics=("arbitrary",), vmem_limit_bytes=_vmem_limit(est)),
        name="in_proj_b",
    )(x, w_in.astype(BF16))


def _win_attn_kernel(slopes_ref, sink_ref, q_ref, k_ref, vt_ref, o_ref,
                     rel_ref, w_ref, st_ref, max_ref, p_ref, linv_ref, ot_ref, *, seq):
    i = pl.program_id(1)
    span = 3 * WINDOW
    nblk = WIN_TQ // WINDOW
    pair = 2 * HEAD_DIM
    assert WINDOW == V7X_LANES and nblk >= 4

    rel_ref[...] = (lax.broadcasted_iota(jnp.int32, (span, WINDOW), 0)
                    - lax.broadcasted_iota(jnp.int32, (span, WINDOW), 1)).astype(F32)

    def rows_of(n):
        return pl.ds(n * WINDOW, WINDOW) if isinstance(n, int) else pl.ds(pl.multiple_of(n * WINDOW, WINDOW), WINDOW)

    def window_of(n):
        gn = i * nblk + n
        return jnp.clip(gn - 1, 0, seq // WINDOW - 3), gn

    def prep_q(n):
        qt = q_ref[rows_of(n), :].astype(F32).T
        zeros = jnp.zeros((HEAD_DIM, WINDOW), F32)
        for g in range(WIN_KV_HEADS):
            cols = []
            for hh in range(WIN_GROUP):
                x = qt[(g * WIN_GROUP + hh) * HEAD_DIM:(g * WIN_GROUP + hh + 1) * HEAD_DIM, :]
                cols.append(jnp.concatenate([x, zeros] if g % 2 == 0 else [zeros, x], axis=0))
            w_ref[g] = jnp.concatenate(cols, axis=1).astype(BF16)

    def scores(n, g):
        c0, gn = window_of(n)
        keys = k_ref[pl.ds(pl.multiple_of(c0 * WINDOW, WINDOW), span), (g // 2) * pair:(g // 2 + 1) * pair]
        st = jnp.dot(keys, w_ref[g], preferred_element_type=F32)
        dist = jnp.abs(((gn - c0) * WINDOW).astype(F32) - rel_ref[...])
        penalty = jnp.where(dist <= WINDOW, 0.0, -NEG)
        for hh in range(WIN_GROUP):
            head = g * WIN_GROUP + hh
            lanes = slice(hh * WINDOW, (hh + 1) * WINDOW)
            s = st[:, lanes] - (slopes_ref[head] * dist + penalty)
            st_ref[g, :, lanes] = s
            max_ref[g, :, lanes] = jnp.maximum(jnp.max(s, axis=0, keepdims=True), sink_ref[head])

    def softmax(n, g):
        for hh in range(WIN_GROUP):
            head = g * WIN_GROUP + hh
            lanes = slice(hh * WINDOW, (hh + 1) * WINDOW)
            m = max_ref[g, :, lanes]
            p = jnp.exp2(st_ref[g, :, lanes] - m)
            p_ref[g, :, lanes] = p.astype(BF16)
            linv_ref[g, :, lanes] = 1.0 / (jnp.sum(p, axis=0, keepdims=True) + jnp.exp2(sink_ref[head] - m))

    def pv(n, g):
        c0, _ = window_of(n)
        vt = jnp.concatenate([vt_ref[c0 + e, g * HEAD_DIM:(g + 1) * HEAD_DIM, :] for e in range(3)], axis=1)
        ot_ref[g] = jnp.dot(vt, p_ref[g], preferred_element_type=F32) * linv_ref[g]

    def assemble(n):
        for pp in range(WIN_Q_HEADS // 2):
            parts = []
            for head in (2 * pp, 2 * pp + 1):
                g, hh = divmod(head, WIN_GROUP)
                parts.append(ot_ref[g, :, hh * WINDOW:(hh + 1) * WINDOW])
            o_ref[rows_of(n), pp * pair:(pp + 1) * pair] = jnp.concatenate(parts, axis=0).T.astype(BF16)

    def iteration(n, do_assemble, do_pv, do_softmax, do_scores):
        nxt = min(n + 1, nblk - 1) if isinstance(n, int) else jnp.minimum(n + 1, nblk - 1)
        if do_assemble:
            assemble(n - 2)
        if do_scores:
            prep_q(nxt)
        for g in range(WIN_KV_HEADS):
            if do_pv:
                pv(n - 1, g)
            if do_softmax:
                softmax(n, g)
            if do_scores:
                scores(nxt, g)

    prep_q(0)
    for g in range(WIN_KV_HEADS):
        scores(0, g)
    iteration(0, False, False, True, True)
    iteration(1, False, True, True, True)

    def body(n, carry):
        iteration(n, True, True, True, True)
        return carry

    lax.fori_loop(2, nblk, body, 0)
    iteration(nblk, True, True, False, False)
    iteration(nblk + 1, True, False, False, False)


def _win_attn(q, k, vt, sink, batch, seq):
    nq = seq // WIN_TQ
    span = 3 * WINDOW
    slopes = _alibi_slopes(WIN_Q_HEADS) * LOG2E
    est = (2 * 2 * seq * WIN_KV_WIDTH * 2 + 4 * WIN_TQ * TOK_WIDTH * 2
           + WIN_KV_HEADS * span * span * (4 + 2) + 8 * span * span * 4)
    return pl.pallas_call(
        functools.partial(_win_attn_kernel, seq=seq),
        out_shape=jax.ShapeDtypeStruct((batch * seq, TOK_WIDTH), BF16),
        grid=(batch, nq),
        in_specs=[
            pl.BlockSpec(memory_space=pltpu.SMEM),
            pl.BlockSpec(memory_space=pltpu.SMEM),
            pl.BlockSpec((WIN_TQ, TOK_WIDTH), lambda b, i: (b * nq + i, 0)),
            pl.BlockSpec((seq, WIN_KV_WIDTH), lambda b, i: (b, 0)),
            pl.BlockSpec((None, seq // WINDOW, WIN_KV_WIDTH, WINDOW), lambda b, i: (b, 0, 0, 0)),
        ],
        out_specs=pl.BlockSpec((WIN_TQ, TOK_WIDTH), lambda b, i: (b * nq + i, 0)),
        scratch_shapes=[
            pltpu.VMEM((span, WINDOW), F32),
            pltpu.VMEM((WIN_KV_HEADS, 2 * HEAD_DIM, span), BF16),
            pltpu.VMEM((WIN_KV_HEADS, span, span), F32),
            pltpu.VMEM((WIN_KV_HEADS, 1, span), F32),
            pltpu.VMEM((WIN_KV_HEADS, span, span), BF16),
            pltpu.VMEM((WIN_KV_HEADS, 1, span), F32),
            pltpu.VMEM((WIN_KV_HEADS, HEAD_DIM, span), F32),
        ],
        compiler_params=pltpu.CompilerParams(
            dimension_semantics=("arbitrary", "arbitrary"), vmem_limit_bytes=_vmem_limit(est)),
        name="win_attn",
    )(slopes, sink.astype(F32) * LOG2E, q, k, vt)


def kernel(x, mem, ffn1_w13, ffn1_w2, ln1_g, ln1_b, w_mem_kv, w_out, ln2_g, ln2_b, ffn2_w13, ffn2_w2,
           ln3_g, ln3_b, a_w_in, a_lambda_q1, a_lambda_k1, a_lambda_q2, a_lambda_k2, a_subln_g, b_w_in, b_sink):
    batch, seq, _ = x.shape
    xf = x.reshape(batch * seq, D_MODEL)
    w_out = w_out.astype(BF16)
    for i in range(DEPTH):
        xf = _ffn_ln(xf, ffn1_w13, ffn1_w2, i, ln1_g[i], ln1_b[i])
        j = i // N_MIXERS
        kt, v = _mem_kv(mem, w_mem_kv[i])
        if i % N_MIXERS == 0:
            lam_init = 0.8 - 0.6 * math.exp(-0.3 * i)
            q, k, vt, qm = _in_proj_a(xf, a_w_in[j], batch, seq)
            o = _diff_attn(q, k, vt, a_lambda_q1[j], a_lambda_k1[j], a_lambda_q2[j], a_lambda_k2[j],
                           a_subln_g[j], lam_init, batch, seq)
        else:
            q, k, vt, qm = _in_proj_b(xf, b_w_in[j], batch, seq)
            o = _win_attn(q, k, vt, b_sink[j], batch, seq)
        xf = _mix_out(xf, o, qm, kt, v, w_out, i, ln2_g[i], ln2_b[i], seq)
        xf = _ffn_ln(xf, ffn2_w13, ffn2_w2, i, ln3_g[i], ln3_b[i])
    return xf.reshape(batch, seq, D_MODEL)
```
